```python
import jax, jax.numpy as jnp
from jax import lax
import numpy as np

D_MODEL = 1024
BATCH = 8
SEQ = 2048
DEPTH = 4
DEC_BATCH = 128
DEC_SEQ = 1
PAST_LEN = 16384
PAGE_SIZE = 128

N_MIXERS = 2
N_RET_LAYERS = (DEPTH + 1) // 2
N_SSM_LAYERS = DEPTH // 2

RET_HEADS = 4
RET_DK = D_MODEL // RET_HEADS
RET_DV = 2 * RET_DK
RET_CHUNK = 128
RET_IN = RET_HEADS * (2 * RET_DK + 2 * RET_DV)
ROPE_BASE = 10000.0

SSM_GROUP = 16
SSM_GROUPS = D_MODEL // SSM_GROUP
SSM_STATE = 64
DT_MIN = 1e-3
DT_MAX = 1e-1

PEER_HEADS = 8
PEER_NKEYS = 128
PEER_EXPERTS = PEER_NKEYS * PEER_NKEYS
PEER_DQ = 256
PEER_TOPK = 16
PEER_BLOCK = 128

EPS = 1e-6

kernel_name = "retention_s5_peer_hybrid_step"


def rmsnorm(x, g):
    xf = x.astype(jnp.float32)
    y = xf * lax.rsqrt(jnp.mean(xf * xf, axis=-1, keepdims=True) + EPS)
    return (y * g.astype(jnp.float32)).astype(x.dtype)


def modulate(h, shift, scale):
    return h * (1 + scale[:, None, :]) + shift[:, None, :]


def rotary(x, pos):
    half = x.shape[-1] // 2
    inv = ROPE_BASE ** (-jnp.arange(half, dtype=jnp.float32) / half)
    ang = pos.astype(jnp.float32)[:, None] * inv[None, :]
    cos = jnp.cos(ang)[None, :, None, :]
    sin = jnp.sin(ang)[None, :, None, :]
    x1 = x[..., :half].astype(jnp.float32)
    x2 = x[..., half:].astype(jnp.float32)
    return jnp.concatenate([x1 * cos - x2 * sin, x2 * cos + x1 * sin], axis=-1).astype(x.dtype)


def retention_scan(q, k, v, s0):
    B, S, H, _ = q.shape
    L = RET_CHUNK if S % RET_CHUNK == 0 else S
    nc = S // L
    lg = jnp.log1p(-jnp.exp2(-5.0 - jnp.arange(H, dtype=jnp.float32)))
    i = jnp.arange(L, dtype=jnp.float32)
    diff = i[:, None] - i[None, :]
    dmask = jnp.where(diff >= 0, jnp.exp(lg[:, None, None] * jnp.maximum(diff, 0.0)), 0.0)
    cross = jnp.exp(lg[:, None] * (i[None, :] + 1.0))[None, :, :, None]
    tail = jnp.exp(lg[:, None] * (L - 1.0 - i[None, :]))[None, :, :, None]
    chunk_decay = jnp.exp(lg * L)[None, :, None, None]

    def to_chunks(t):
        return t.reshape(B, nc, L, H, t.shape[-1]).transpose(1, 0, 3, 2, 4).astype(jnp.float32)

    def step(s, qkv):
        qc, kc, vc = qkv
        att = jnp.einsum('bhid,bhjd->bhij', qc, kc) * dmask
        o = jnp.einsum('bhij,bhjv->bhiv', att, vc) + jnp.einsum('bhid,bhdv->bhiv', qc, s) * cross
        s = s * chunk_decay + jnp.einsum('bhjd,bhjv->bhdv', kc * tail, vc)
        return s, o

    s, o = lax.scan(step, s0.astype(jnp.float32), (to_chunks(q), to_chunks(k), to_chunks(v)))
    o = o.transpose(1, 0, 3, 2, 4).reshape(B, S, H, -1)
    return o, s


def retention_mixer(h, pos, s0, w_in, w_out):
    B, S, _ = h.shape
    proj = h @ w_in
    nq = RET_HEADS * RET_DK
    nv = RET_HEADS * RET_DV
    q, k, v, g = jnp.split(proj, [nq, 2 * nq, 2 * nq + nv], axis=-1)
    q = rotary(q.reshape(B, S, RET_HEADS, RET_DK), pos)
    k = rotary(k.reshape(B, S, RET_HEADS, RET_DK), pos) * (RET_DK ** -0.5)
    v = v.reshape(B, S, RET_HEADS, RET_DV)
    o, s = retention_scan(q, k, v, s0)
    mu = jnp.mean(o, axis=-1, keepdims=True)
    var = jnp.mean(jnp.square(o - mu), axis=-1, keepdims=True)
    o = ((o - mu) * lax.rsqrt(var + EPS)).reshape(B, S, RET_HEADS * RET_DV).astype(h.dtype)
    return (jax.nn.silu(g) * o) @ w_out, s


def ssm_discretize(lam_re, lam_im, log_dt, b_re, b_im):
    dt = jnp.exp(log_dt.astype(jnp.float32))[:, None]
    lr = lam_re.astype(jnp.float32)
    li = lam_im.astype(jnp.float32)
    mag = jnp.exp(lr * dt)
    a_re = mag * jnp.cos(li * dt)
    a_im = mag * jnp.sin(li * dt)
    den = lr * lr + li * li
    nr = a_re - 1.0
    c_re = ((nr * lr + a_im * li) / den)[..., None]
    c_im = ((a_im * lr - nr * li) / den)[..., None]
    br = b_re.astype(jnp.float32)
    bi = b_im.astype(jnp.float32)
    return a_re, a_im, c_re * br - c_im * bi, c_re * bi + c_im * br


def ssm_scan(u, h0_re, h0_im, a_re, a_im, bb_re, bb_im):
    bu_re = jnp.einsum('bsgp,gnp->bsgn', u, bb_re)
    bu_im = jnp.einsum('bsgp,gnp->bsgn', u, bb_im)
    h0r = h0_re.astype(jnp.float32)
    h0i = h0_im.astype(jnp.float32)
    bu_re = bu_re.at[:, 0].add(a_re * h0r - a_im * h0i)
    bu_im = bu_im.at[:, 0].add(a_re * h0i + a_im * h0r)
    A_re = jnp.broadcast_to(a_re, bu_re.shape)
    A_im = jnp.broadcast_to(a_im, bu_im.shape)

    def combine(e1, e2):
        ar1, ai1, br1, bi1 = e1
        ar2, ai2, br2, bi2 = e2
        return (ar2 * ar1 - ai2 * ai1, ar2 * ai1 + ai2 * ar1,
                ar2 * br1 - ai2 * bi1 + br2, ar2 * bi1 + ai2 * br1 + bi2)

    _, _, h_re, h_im = lax.associative_scan(combine, (A_re, A_im, bu_re, bu_im), axis=1)
    return h_re, h_im


def ssm_mixer(h, h0_re, h0_im, w_in, lam_re, lam_im, log_dt, b_re, b_im, c_re, c_im, d_skip, w_glu):
    B, S, D = h.shape
    u = (h @ w_in).astype(jnp.float32).reshape(B, S, SSM_GROUPS, SSM_GROUP)
    a_re, a_im, bb_re, bb_im = ssm_discretize(lam_re, lam_im, log_dt, b_re, b_im)
    hs_re, hs_im = ssm_scan(u, h0_re, h0_im, a_re, a_im, bb_re, bb_im)
    y = (jnp.einsum('bsgn,gpn->bsgp', hs_re, c_re.astype(jnp.float32))
         - jnp.einsum('bsgn,gpn->bsgp', hs_im, c_im.astype(jnp.float32))
         + d_skip.astype(jnp.float32) * u)
    z = jax.nn.gelu(y.reshape(B, S, D)).astype(h.dtype)
    a, gt = jnp.split(z @ w_glu, 2, axis=-1)
    return a * jax.nn.sigmoid(gt), hs_re[:, -1], hs_im[:, -1]


def peer_block(xb, w_q, key1, key2, u_tab, v_tab):
    T = xb.shape[0]
    q = (xb @ w_q).astype(jnp.float32).reshape(T, PEER_HEADS, 2, PEER_DQ // 2)
    s1 = jnp.einsum('thd,hkd->thk', q[:, :, 0], key1.astype(jnp.float32))
    s2 = jnp.einsum('thd,hkd->thk', q[:, :, 1], key2.astype(jnp.float32))
    v1, i1 = lax.top_k(s1, PEER_TOPK)
    v2, i2 = lax.top_k(s2, PEER_TOPK)
    cand = (v1[..., :, None] + v2[..., None, :]).reshape(T, PEER_HEADS, PEER_TOPK * PEER_TOPK)
    cidx = (i1[..., :, None] * PEER_NKEYS + i2[..., None, :]).reshape(T, PEER_HEADS, PEER_TOPK * PEER_TOPK)
    top, sel = lax.top_k(cand, PEER_TOPK)
    eidx = jnp.take_along_axis(cidx, sel, axis=-1)
    gate = jax.nn.softmax(top, axis=-1)
    u = jnp.take(u_tab, eidx, axis=0)
    act = jax.nn.gelu(jnp.einsum('thkd,td->thk', u, xb).astype(jnp.float32))
    w = (gate * act).astype(xb.dtype)
    v = jnp.take(v_tab, eidx, axis=0)
    return jnp.einsum('thk,thkd->td', w, v)


def peer(x, w_q, key1, key2, u_tab, v_tab):
    B, S, D = x.shape
    T = B * S
    nb = -(-T // PEER_BLOCK)
    xf = jnp.pad(x.reshape(T, D), ((0, nb * PEER_BLOCK - T), (0, 0)))
    out = lax.map(lambda xb: peer_block(xb, w_q, key1, key2, u_tab, v_tab), xf.reshape(nb, PEER_BLOCK, D))
    return out.reshape(nb * PEER_BLOCK, D)[:T].reshape(B, S, D)


def setup_inputs(seed: int = 0) -> dict:
    key = jax.random.key(seed)
    ks = jax.random.split(key, 32)
    f32 = jnp.float32
    D = D_MODEL
    nrm = lambda k, shape, s: jax.random.normal(k, shape, f32) * s
    n_idx = jnp.arange(SSM_STATE, dtype=f32)
    return {
        "x_prompt": nrm(ks[0], (BATCH, SEQ, D), 1.0),
        "x_sample": nrm(ks[1], (DEC_BATCH, DEC_SEQ, D), 1.0),
        "c_prompt": nrm(ks[2], (BATCH, D), 1.0),
        "c_sample": nrm(ks[3], (DEC_BATCH, D), 1.0),
        "state_ret": nrm(ks[4], (N_RET_LAYERS, DEC_BATCH, RET_HEADS, RET_DK, RET_DV), 0.3),
        "state_ssm_re": nrm(ks[5], (N_SSM_LAYERS, DEC_BATCH, SSM_GROUPS, SSM_STATE), 0.5),
        "state_ssm_im": nrm(ks[6], (N_SSM_LAYERS, DEC_BATCH, SSM_GROUPS, SSM_STATE), 0.5),
        "norm_g": 1.0 + nrm(ks[7], (DEPTH, 2, D), 0.01),
        "final_g": 1.0 + nrm(ks[8], (D,), 0.01),
        "w_ada": nrm(ks[9], (DEPTH, D, 6 * D), 0.5 * D ** -0.5),
        "b_ada": nrm(ks[10], (DEPTH, 6 * D), 0.02),
        "ret_w_in": nrm(ks[11], (N_RET_LAYERS, D, RET_IN), D ** -0.5),
        "ret_w_out": nrm(ks[12], (N_RET_LAYERS, RET_HEADS * RET_DV, D), (RET_HEADS * RET_DV) ** -0.5),
        "ssm_w_in": nrm(ks[13], (N_SSM_LAYERS, D, D), D ** -0.5),
        "ssm_lam_re": -0.5 + nrm(ks[14], (N_SSM_LAYERS, SSM_GROUPS, SSM_STATE), 0.01),
        "ssm_lam_im": jnp.pi * n_idx + nrm(ks[15], (N_SSM_LAYERS, SSM_GROUPS, SSM_STATE), 0.01),
        "ssm_log_dt": jax.random.uniform(ks[16], (N_SSM_LAYERS, SSM_GROUPS), f32, float(np.log(DT_MIN)), float(np.log(DT_MAX))),
        "ssm_b_re": nrm(ks[17], (N_SSM_LAYERS, SSM_GROUPS, SSM_STATE, SSM_GROUP), (2 * SSM_GROUP) ** -0.5),
        "ssm_b_im": nrm(ks[18], (N_SSM_LAYERS, SSM_GROUPS, SSM_STATE, SSM_GROUP), (2 * SSM_GROUP) ** -0.5),
        "ssm_c_re": nrm(ks[19], (N_SSM_LAYERS, SSM_GROUPS, SSM_GROUP, SSM_STATE), (2 * SSM_STATE) ** -0.5),
        "ssm_c_im": nrm(ks[20], (N_SSM_LAYERS, SSM_GROUPS, SSM_GROUP, SSM_STATE), (2 * SSM_STATE) ** -0.5),
        "ssm_d": nrm(ks[21], (N_SSM_LAYERS, SSM_GROUPS, SSM_GROUP), 1.0),
        "ssm_w_glu": nrm(ks[22], (N_SSM_LAYERS, D, 2 * D), D ** -0.5),
        "peer_w_q": nrm(ks[23], (DEPTH, D, PEER_HEADS * PEER_DQ), D ** -0.5),
        "peer_key1": nrm(ks[24], (DEPTH, PEER_HEADS, PEER_NKEYS, PEER_DQ // 2), (PEER_DQ // 2) ** -0.5),
        "peer_key2": nrm(ks[25], (DEPTH, PEER_HEADS, PEER_NKEYS, PEER_DQ // 2), (PEER_DQ // 2) ** -0.5),
        "peer_u": nrm(ks[26], (DEPTH, PEER_EXPERTS, D), D ** -0.5),
        "peer_v": nrm(ks[27], (DEPTH, PEER_EXPERTS, D), PEER_HEADS ** -0.5),
    }


def reference(x_prompt, x_sample, c_prompt, c_sample, state_ret, state_ssm_re, state_ssm_im,
              norm_g, final_g, w_ada, b_ada, ret_w_in, ret_w_out,
              ssm_w_in, ssm_lam_re, ssm_lam_im, ssm_log_dt, ssm_b_re, ssm_b_im, ssm_c_re, ssm_c_im, ssm_d, ssm_w_glu,
              peer_w_q, peer_key1, peer_key2, peer_u, peer_v):

    def run(x, c, pos, ret_s0, ssm_re0, ssm_im0):
        new_ret, new_re, new_im = [], [], []
        for i in range(DEPTH):
            mod = jax.nn.silu(c) @ w_ada[i] + b_ada[i]
            sh1, sc1, g1, sh2, sc2, g2 = jnp.split(mod, 6, axis=-1)
            h = modulate(rmsnorm(x, norm_g[i, 0]), sh1, sc1)
            j = i // N_MIXERS
            if i % N_MIXERS == 0:
                y, s = retention_mixer(h, pos, ret_s0[j], ret_w_in[j], ret_w_out[j])
                new_ret.append(s.astype(x.dtype))
            else:
                y, hr, hi = ssm_mixer(h, ssm_re0[j], ssm_im0[j], ssm_w_in[j], ssm_lam_re[j], ssm_lam_im[j],
                                      ssm_log_dt[j], ssm_b_re[j], ssm_b_im[j], ssm_c_re[j], ssm_c_im[j],
                                      ssm_d[j], ssm_w_glu[j])
                new_re.append(hr.astype(x.dtype))
                new_im.append(hi.astype(x.dtype))
            x = x + g1[:, None, :] * y
            h = modulate(rmsnorm(x, norm_g[i, 1]), sh2, sc2)
            x = x + g2[:, None, :] * peer(h, peer_w_q[i], peer_key1[i], peer_key2[i], peer_u[i], peer_v[i])
        return rmsnorm(x, final_g), jnp.stack(new_ret), jnp.stack(new_re), jnp.stack(new_im)

    B = x_prompt.shape[0]
    dt = x_prompt.dtype
    pos_p = jnp.arange(x_prompt.shape[1], dtype=jnp.int32)
    pos_s = PAST_LEN + jnp.arange(x_sample.shape[1], dtype=jnp.int32)
    ret0_p = jnp.zeros((N_RET_LAYERS, B, RET_HEADS, RET_DK, RET_DV), dt)
    ssm0_p = jnp.zeros((N_SSM_LAYERS, B, SSM_GROUPS, SSM_STATE), dt)
    y_prompt, ret_prompt, ssm_re_prompt, ssm_im_prompt = run(x_prompt, c_prompt, pos_p, ret0_p, ssm0_p, ssm0_p)
    y_sample, ret_sample, ssm_re_sample, ssm_im_sample = run(x_sample, c_sample, pos_s, state_ret, state_ssm_re, state_ssm_im)
    return (y_prompt, y_sample, ret_prompt, ret_sample, ssm_re_prompt, ssm_im_prompt, ssm_re_sample, ssm_im_sample)
```

```python
import functools
import math

import jax
import jax.numpy as jnp
import numpy as np
from jax import lax
from jax.experimental import pallas as pl
from jax.experimental.pallas import tpu as pltpu

F32 = jnp.float32
BF16 = jnp.bfloat16

EPS = 1e-6
ROPE_BASE = 10000.0
PAST_LEN = 16384
RET_HEADS = 4
RET_CHUNK = 128
SSM_GROUP = 16
SSM_STATE = 64
PEER_HEADS = 8
PEER_NKEYS = 128
PEER_TOPK = 16
N_MIXERS = 2

V7X_LANES = 128
V7X_SUBLANES = 8
V7X_MXU_DIM = 256
V7X_VMEM_LIMIT_BYTES = 56 * 1024 * 1024

NEG_INF = float("-inf")


def _cparams(*sem):
    return pltpu.CompilerParams(dimension_semantics=sem, vmem_limit_bytes=V7X_VMEM_LIMIT_BYTES)


def _dot(a, b):
    return jnp.dot(a, b, preferred_element_type=F32)


def _dot_nt(a, b):
    return lax.dot_general(a, b, (((1,), (1,)), ((), ())), preferred_element_type=F32)


def _dot_tn(a, b):
    return lax.dot_general(a, b, (((0,), (0,)), ((), ())), preferred_element_type=F32)


def _sigmoid(x):
    return 1.0 / (1.0 + jnp.exp(-x))


def _gelu_tanh(x):
    c = math.sqrt(2.0 / math.pi)
    return 0.5 * x * (1.0 + jnp.tanh(c * (x + 0.044715 * (x * x * x))))


def _row_tile(t, pref):
    if t <= pref:
        return t
    tile = pref
    while t % tile:
        tile //= 2
    return tile


def _ada_kernel(c_ref, w_ref, b_ref, o_ref):
    c = c_ref[...]
    sc = (c * _sigmoid(c)).astype(BF16)
    o_ref[0] = _dot(sc, w_ref[0]) + b_ref[0]


def _ada(c_all, w_ada, b_ada):
    m, d = c_all.shape
    depth, _, n = w_ada.shape
    tn = 1024
    return pl.pallas_call(
        _ada_kernel,
        out_shape=jax.ShapeDtypeStruct((depth, m, n), F32),
        grid=(depth, n // tn),
        in_specs=[
            pl.BlockSpec((m, d), lambda l, j: (0, 0)),
            pl.BlockSpec((1, d, tn), lambda l, j: (l, 0, j)),
            pl.BlockSpec((1, 1, tn), lambda l, j: (l, 0, j)),
        ],
        out_specs=pl.BlockSpec((1, m, tn), lambda l, j: (l, 0, j)),
        compiler_params=_cparams("parallel", "parallel"),
        name="ada_mod",
    )(c_all, w_ada, b_ada.reshape(depth, 1, n))


def _norm_mod(x, g, sh, sc):
    y = x * lax.rsqrt(jnp.mean(x * x, axis=-1, keepdims=True) + EPS)
    return (y * g) * (1.0 + sc) + sh


def _nm_matmul_kernel(x_ref, g_ref, sh_ref, sc_ref, w_ref, o_ref, *rest, emit_h):
    if emit_h:
        h_ref, h_scr = rest
    else:
        (h_scr,) = rest
    j = pl.program_id(1)

    @pl.when(j == 0)
    def _():
        h = _norm_mod(x_ref[...], g_ref[...], sh_ref[0], sc_ref[0]).astype(BF16)
        h_scr[...] = h
        if emit_h:
            h_ref[...] = h

    o_ref[...] = _dot(h_scr[...], w_ref[...])


def _mod_specs(tm, d, seq):
    if seq == 1:
        return pl.BlockSpec((1, tm, d), lambda i, *_: (0, i, 0))
    per = seq // tm
    return pl.BlockSpec((1, 1, d), lambda i, *_: (i // per, 0, 0))


def _nm_matmul(x, g, sh, sc, w, seq, emit_h=False, tm_pref=512, tn_pref=1024):
    t, d = x.shape
    n = w.shape[1]
    tm = _row_tile(min(t, seq) if seq > 1 else t, tm_pref)
    tn = min(n, tn_pref)
    out_shape = [jax.ShapeDtypeStruct((t, n), F32)]
    out_specs = [pl.BlockSpec((tm, tn), lambda i, j: (i, j))]
    if emit_h:
        out_shape.append(jax.ShapeDtypeStruct((t, d), BF16))
        out_specs.append(pl.BlockSpec((tm, d), lambda i, j: (i, 0)))
    res = pl.pallas_call(
        functools.partial(_nm_matmul_kernel, emit_h=emit_h),
        out_shape=out_shape,
        grid=(t // tm, n // tn),
        in_specs=[
            pl.BlockSpec((tm, d), lambda i, j: (i, 0)),
            pl.BlockSpec((1, d), lambda i, j: (0, 0)),
            _mod_specs(tm, d, seq),
            _mod_specs(tm, d, seq),
            pl.BlockSpec((d, tn), lambda i, j: (0, j)),
        ],
        out_specs=out_specs,
        scratch_shapes=[pltpu.VMEM((tm, d), BF16)],
        compiler_params=_cparams("parallel", "arbitrary"),
        name="norm_mod_matmul",
    )(x, g, sh, sc, w)
    return res if emit_h else res[0]


def _matmul_res_kernel(a_ref, w_ref, x_ref, gate_ref, o_ref, *, glu):
    r = _dot(a_ref[...], w_ref[...])
    if glu:
        half = r.shape[1] // 2
        r = r[:, :half] * _sigmoid(r[:, half:])
    o_ref[...] = x_ref[...] + gate_ref[0] * r


def _matmul_res(a, w, x, gate, seq, glu, tm_pref=512):
    t, k = a.shape
    n = w.shape[1]
    d = x.shape[1]
    tm = _row_tile(min(t, seq) if seq > 1 else t, tm_pref)
    return pl.pallas_call(
        functools.partial(_matmul_res_kernel, glu=glu),
        out_shape=jax.ShapeDtypeStruct((t, d), F32),
        grid=(t // tm,),
        in_specs=[
            pl.BlockSpec((tm, k), lambda i: (i, 0)),
            pl.BlockSpec((k, n), lambda i: (0, 0)),
            pl.BlockSpec((tm, d), lambda i: (i, 0)),
            _mod_specs(tm, d, seq),
        ],
        out_specs=pl.BlockSpec((tm, d), lambda i: (i, 0)),
        compiler_params=_cparams("parallel"),
        name="matmul_residual",
    )(a, w, x, gate)


def _rotary(x, cos, sin):
    half = x.shape[-1] // 2
    x1 = x[:, :half]
    x2 = x[:, half:]
    return jnp.concatenate([x1 * cos - x2 * sin, x2 * cos + x1 * sin], axis=-1)


def _group_norm_gate(o, g):
    mu = jnp.mean(o, axis=-1, keepdims=True)
    var = jnp.mean(jnp.square(o - mu), axis=-1, keepdims=True)
    on = (o - mu) * lax.rsqrt(var + EPS)
    return (g * _sigmoid(g)) * on


def _ret_chunk_kernel(q_ref, k_ref, v_ref, g_ref, cos_ref, sin_ref, dmask_ref, cross_ref, tail_ref,
                      cdec_ref, o_ref, sfin_ref, s_scr, *, n_sub, chunk, k_scale):
    c = pl.program_id(2)

    @pl.when(c == 0)
    def _():
        s_scr[...] = jnp.zeros_like(s_scr)

    dmask = dmask_ref[0]
    cross = cross_ref[0]
    tail = tail_ref[0]
    cdec = cdec_ref[0]
    for j in range(n_sub):
        rows = pl.ds(j * chunk, chunk)
        cos = cos_ref[rows, :]
        sin = sin_ref[rows, :]
        qc = _rotary(q_ref[rows, :], cos, sin)
        kc = _rotary(k_ref[rows, :], cos, sin) * k_scale
        vb = v_ref[rows, :].astype(BF16)
        qb = qc.astype(BF16)
        att = _dot_nt(qb, kc.astype(BF16)) * dmask
        s = s_scr[...]
        o = _dot(att.astype(BF16), vb) + _dot(qb, s.astype(BF16)) * cross
        s_scr[...] = s * cdec + _dot_tn((kc * tail).astype(BF16), vb)
        o_ref[rows, :] = _group_norm_gate(o, g_ref[rows, :]).astype(BF16)

    @pl.when(c == pl.num_programs(2) - 1)
    def _():
        sfin_ref[0, 0] = s_scr[...]


def _ret_tables(heads, chunk):
    lg = jnp.log1p(-jnp.exp2(-5.0 - jnp.arange(heads, dtype=F32)))
    i = jnp.arange(chunk, dtype=F32)
    diff = i[:, None] - i[None, :]
    dmask = jnp.where(diff >= 0, jnp.exp(lg[:, None, None] * jnp.maximum(diff, 0.0)), 0.0)
    cross = jnp.exp(lg[:, None] * (i[None, :] + 1.0))[:, :, None]
    tail = jnp.exp(lg[:, None] * (chunk - 1.0 - i[None, :]))[:, :, None]
    cdec = jnp.exp(lg * chunk)
    return dmask, cross, tail, cdec


def _rope_tables(pos, half):
    inv = ROPE_BASE ** (-jnp.arange(half, dtype=F32) / half)
    ang = pos.astype(F32)[:, None] * inv[None, :]
    return jnp.cos(ang), jnp.sin(ang)


def _retention_prompt(proj, batch, seq, d):
    heads = RET_HEADS
    dk = d // heads
    dv = 2 * dk
    chunk = RET_CHUNK
    rows = _row_tile(seq, 4 * chunk)
    n_sub = rows // chunk
    nblk = seq // rows
    cos, sin = _rope_tables(jnp.arange(seq, dtype=jnp.int32), dk // 2)
    dmask, cross, tail, cdec = _ret_tables(heads, chunk)
    cdec = jnp.broadcast_to(cdec[:, None, None], (heads, 1, dv))
    kq = heads
    vq = (2 * heads * dk) // dv
    gq = vq + heads
    o, sfin = pl.pallas_call(
        functools.partial(_ret_chunk_kernel, n_sub=n_sub, chunk=chunk, k_scale=dk ** -0.5),
        out_shape=[jax.ShapeDtypeStruct((batch * seq, heads * dv), BF16),
                   jax.ShapeDtypeStruct((batch, heads, dk, dv), F32)],
        grid=(batch, heads, nblk),
        in_specs=[
            pl.BlockSpec((rows, dk), lambda b, h, c: (b * nblk + c, h)),
            pl.BlockSpec((rows, dk), lambda b, h, c: (b * nblk + c, kq + h)),
            pl.BlockSpec((rows, dv), lambda b, h, c: (b * nblk + c, vq + h)),
            pl.BlockSpec((rows, dv), lambda b, h, c: (b * nblk + c, gq + h)),
            pl.BlockSpec((rows, dk // 2), lambda b, h, c: (c, 0)),
            pl.BlockSpec((rows, dk // 2), lambda b, h, c: (c, 0)),
            pl.BlockSpec((1, chunk, chunk), lambda b, h, c: (h, 0, 0)),
            pl.BlockSpec((1, chunk, 1), lambda b, h, c: (h, 0, 0)),
            pl.BlockSpec((1, chunk, 1), lambda b, h, c: (h, 0, 0)),
            pl.BlockSpec((1, 1, dv), lambda b, h, c: (h, 0, 0)),
        ],
        out_specs=[
            pl.BlockSpec((rows, dv), lambda b, h, c: (b * nblk + c, h)),
            pl.BlockSpec((1, 1, dk, dv), lambda b, h, c: (b, h, 0, 0)),
        ],
        scratch_shapes=[pltpu.VMEM((dk, dv), F32)],
        compiler_params=_cparams("parallel", "parallel", "arbitrary"),
        name="retention_chunks",
    )(proj, proj, proj, proj, cos, sin, dmask, cross, tail, cdec)
    return o, sfin


def _ret_step_kernel(q_ref, k_ref, v_ref, g_ref, cos_ref, sin_ref, gam_ref, s0_ref, o_ref, s_ref, *,
                     nb, k_scale):
    cos = cos_ref[...]
    sin = sin_ref[...]
    gam = gam_ref[0]
    qc = _rotary(q_ref[...], cos, sin)
    kc = _rotary(k_ref[...], cos, sin) * k_scale
    qr = qc.astype(BF16).astype(F32)
    kr = kc.astype(BF16).astype(F32)
    vr = v_ref[...].astype(BF16).astype(F32)
    att = jnp.sum(qr * kr, axis=-1, keepdims=True)
    att = att.astype(BF16).astype(F32)
    dk = qc.shape[1]
    eye = (lax.broadcasted_iota(jnp.int32, (dk, dk), 0) == lax.broadcasted_iota(jnp.int32, (dk, dk), 1))
    rows8 = lax.broadcasted_iota(jnp.int32, (V7X_SUBLANES, dk), 0)
    o = jnp.zeros(vr.shape, F32)
    brow = lax.broadcasted_iota(jnp.int32, vr.shape, 0)
    for b in range(nb):
        s0 = s0_ref[b, 0]
        qrow = qc[b:b + 1, :]
        q8 = jnp.where(rows8 == 0, qrow, 0.0).astype(BF16)
        qs = _dot(q8, s0.astype(BF16))[0:1, :]
        o = jnp.where(brow == b, att[b:b + 1, :] * vr[b:b + 1, :] + qs * gam, o)
        kcol = jnp.sum(jnp.where(eye, kr[b:b + 1, :], 0.0), axis=1, keepdims=True)
        s_ref[b, 0] = s0 * gam + kcol * vr[b:b + 1, :]
    o_ref[...] = _group_norm_gate(o, g_ref[...]).astype(BF16)


def _retention_step(proj, state, d):
    t = proj.shape[0]
    heads = RET_HEADS
    dk = d // heads
    dv = 2 * dk
    nb = V7X_SUBLANES
    cos, sin = _rope_tables(PAST_LEN + jnp.arange(1, dtype=jnp.int32), dk // 2)
    lg = jnp.log1p(-jnp.exp2(-5.0 - jnp.arange(heads, dtype=F32)))
    gam = jnp.broadcast_to(jnp.exp(lg)[:, None, None], (heads, 1, dv))
    kq = heads
    vq = (2 * heads * dk) // dv
    gq = vq + heads
    o, s = pl.pallas_call(
        functools.partial(_ret_step_kernel, nb=nb, k_scale=dk ** -0.5),
        out_shape=[jax.ShapeDtypeStruct((t, heads * dv), BF16),
                   jax.ShapeDtypeStruct(state.shape, F32)],
        grid=(t // nb, heads),
        in_specs=[
            pl.BlockSpec((nb, dk), lambda i, h: (i, h)),
            pl.BlockSpec((nb, dk), lambda i, h: (i, kq + h)),
            pl.BlockSpec((nb, dv), lambda i, h: (i, vq + h)),
            pl.BlockSpec((nb, dv), lambda i, h: (i, gq + h)),
            pl.BlockSpec((1, dk // 2), lambda i, h: (0, 0)),
            pl.BlockSpec((1, dk // 2), lambda i, h: (0, 0)),
            pl.BlockSpec((1, 1, dv), lambda i, h: (h, 0, 0)),
            pl.BlockSpec((nb, 1, dk, dv), lambda i, h: (i, h, 0, 0)),
        ],
        out_specs=[
            pl.BlockSpec((nb, dv), lambda i, h: (i, h)),
            pl.BlockSpec((nb, 1, dk, dv), lambda i, h: (i, h, 0, 0)),
        ],
        compiler_params=_cparams("parallel", "parallel"),
        name="retention_step",
    )(proj, proj, proj, proj, cos, sin, gam, state)
    return o, s


def _ssm_disc_kernel(lr_ref, li_ref, ldt_ref, br_ref, bi_ref, are_ref, aim_ref, bbr_ref, bbi_ref):
    lr = lr_ref[...]
    li = li_ref[...]
    dt = jnp.exp(ldt_ref[...])
    mag = jnp.exp(lr * dt)
    a_re = mag * jnp.cos(li * dt)
    a_im = mag * jnp.sin(li * dt)
    den = lr * lr + li * li
    nr = a_re - 1.0
    c_re = (nr * lr + a_im * li) / den
    c_im = (a_im * lr - nr * li) / den
    are_ref[...] = a_re
    aim_ref[...] = a_im
    br = br_ref[...]
    bi = bi_ref[...]
    bbr_ref[...] = c_re * br - c_im * bi
    bbi_ref[...] = c_re * bi + c_im * br


def _ssm_discretize(lam_re, lam_im, log_dt, b_re, b_im):
    g, n = lam_re.shape
    p = b_re.shape[2]
    brt = jnp.transpose(b_re, (0, 2, 1))
    bit = jnp.transpose(b_im, (0, 2, 1))
    a_re, a_im, bbr, bbi = pl.pallas_call(
        _ssm_disc_kernel,
        out_shape=[jax.ShapeDtypeStruct((g, 1, n), F32), jax.ShapeDtypeStruct((g, 1, n), F32),
                   jax.ShapeDtypeStruct((g, p, n), F32), jax.ShapeDtypeStruct((g, p, n), F32)],
        name="ssm_discretize",
    )(lam_re.reshape(g, 1, n), lam_im.reshape(g, 1, n), log_dt.reshape(g, 1, 1), brt, bit)
    return a_re.reshape(g, n), a_im.reshape(g, n), bbr, bbi


def _block_diag(w, per):
    g, a, b = w.shape
    w = w.reshape(g // per, per, a, b)
    eye = jnp.eye(per, dtype=w.dtype)
    return jnp.einsum('cgab,gh->cgahb', w, eye).reshape(g // per, per * a, per * b)


def _complex_pow_tables(a_re, a_im, rows):
    are = a_re.reshape(1, -1)
    aim = a_im.reshape(1, -1)
    pr, pi = [are], [aim]
    for _ in range(rows - 1):
        r, i = pr[-1], pi[-1]
        pr.append(r * are - i * aim)
        pi.append(r * aim + i * are)
    steps = [1 << k for k in range(int(math.log2(rows)))]
    sr = jnp.concatenate([pr[s - 1] for s in steps], axis=0)
    si = jnp.concatenate([pi[s - 1] for s in steps], axis=0)
    return jnp.concatenate(pr, axis=0), jnp.concatenate(pi, axis=0), sr, si


def _ssm_bu(ub, bdr_ref, bdi_ref, nblk, kw):
    bur = jnp.concatenate([_dot(ub[:, c * kw:(c + 1) * kw], bdr_ref[c]) for c in range(nblk)], axis=1)
    bui = jnp.concatenate([_dot(ub[:, c * kw:(c + 1) * kw], bdi_ref[c]) for c in range(nblk)], axis=1)
    return bur, bui


def _ssm_y(hr, hi, cdr_ref, cdi_ref, nblk, kw):
    hrb = hr.astype(BF16)
    hib = hi.astype(BF16)
    return jnp.concatenate(
        [_dot(hrb[:, c * kw:(c + 1) * kw], cdr_ref[c]) - _dot(hib[:, c * kw:(c + 1) * kw], cdi_ref[c])
         for c in range(nblk)], axis=1)


def _ssm_scan_kernel(u_ref, bdr_ref, bdi_ref, cdr_ref, cdi_ref, d_ref, pr_ref, pi_ref, sr_ref, si_ref,
                     z_ref, fr_ref, fi_ref, hr_scr, hi_scr, cr_scr, ci_scr, *, nblk, ts):
    sub = V7X_SUBLANES
    step = pl.program_id(1)

    @pl.when(step == 0)
    def _():
        cr_scr[...] = jnp.zeros_like(cr_scr)
        ci_scr[...] = jnp.zeros_like(ci_scr)

    u = u_ref[...]
    kw_in = u.shape[1] // nblk
    bur, bui = _ssm_bu(u.astype(BF16), bdr_ref, bdi_ref, nblk, kw_in)
    hr_scr[...] = bur
    hi_scr[...] = bui

    gn = hr_scr.shape[1]
    row = lax.broadcasted_iota(jnp.int32, (sub, gn), 0)
    shifts = [1 << k for k in range(int(math.log2(sub)))]
    masks = [row >= s for s in shifts]

    def group(r, carry):
        cr, ci = carry
        rows = pl.ds(pl.multiple_of(r * sub, sub), sub)
        xr = hr_scr[rows, :]
        xi = hi_scr[rows, :]
        for k, s in enumerate(shifts):
            ar = sr_ref[k:k + 1, :]
            ai = si_ref[k:k + 1, :]
            yr = jnp.where(masks[k], pltpu.roll(xr, s, 0), 0.0)
            yi = jnp.where(masks[k], pltpu.roll(xi, s, 0), 0.0)
            xr, xi = xr + (ar * yr - ai * yi), xi + (ar * yi + ai * yr)
        pr = pr_ref[...]
        pi = pi_ref[...]
        xr, xi = xr + (pr * cr - pi * ci), xi + (pr * ci + pi * cr)
        hr_scr[rows, :] = xr
        hi_scr[rows, :] = xi
        return xr[sub - 1:sub, :], xi[sub - 1:sub, :]

    cr, ci = lax.fori_loop(0, ts // sub, group, (cr_scr[...], ci_scr[...]))
    cr_scr[...] = cr
    ci_scr[...] = ci

    kw_out = gn // nblk
    y = _ssm_y(hr_scr[...], hi_scr[...], cdr_ref, cdi_ref, nblk, kw_out) + d_ref[...] * u
    z_ref[...] = _gelu_tanh(y).astype(BF16)

    @pl.when(step == pl.num_programs(1) - 1)
    def _():
        fr_ref[0] = cr
        fi_ref[0] = ci


def _ssm_prompt(u, disc, batch, seq):
    t, d = u.shape
    bdr, bdi, cdr, cdi, dvec, pr, pi, sr, si = disc
    nblk = bdr.shape[0]
    gn = pr.shape[1]
    ts = _row_tile(seq, 256)
    nstep = seq // ts
    full = lambda a: pl.BlockSpec(a.shape, lambda b, s: (0,) * a.ndim)
    z, fr, fi = pl.pallas_call(
        functools.partial(_ssm_scan_kernel, nblk=nblk, ts=ts),
        out_shape=[jax.ShapeDtypeStruct((t, d), BF16),
                   jax.ShapeDtypeStruct((batch, 1, gn), F32),
                   jax.ShapeDtypeStruct((batch, 1, gn), F32)],
        grid=(batch, nstep),
        in_specs=[pl.BlockSpec((ts, d), lambda b, s: (b * nstep + s, 0)),
                  full(bdr), full(bdi), full(cdr), full(cdi), full(dvec), full(pr), full(pi), full(sr), full(si)],
        out_specs=[pl.BlockSpec((ts, d), lambda b, s: (b * nstep + s, 0)),
                   pl.BlockSpec((1, 1, gn), lambda b, s: (b, 0, 0)),
                   pl.BlockSpec((1, 1, gn), lambda b, s: (b, 0, 0))],
        scratch_shapes=[pltpu.VMEM((ts, gn), F32), pltpu.VMEM((ts, gn), F32),
                        pltpu.VMEM((1, gn), F32), pltpu.VMEM((1, gn), F32)],
        compiler_params=_cparams("parallel", "arbitrary"),
        name="ssm_scan",
    )(u, bdr, bdi, cdr, cdi, dvec, pr, pi, sr, si)
    return z, fr, fi


def _ssm_step_kernel(u_ref, h0r_ref, h0i_ref, bdr_ref, bdi_ref, cdr_ref, cdi_ref, d_ref, ar_ref, ai_ref,
                     z_ref, fr_ref, fi_ref, *, nblk):
    u = u_ref[...]
    kw_in = u.shape[1] // nblk
    bur, bui = _ssm_bu(u.astype(BF16), bdr_ref, bdi_ref, nblk, kw_in)
    ar = ar_ref[...]
    ai = ai_ref[...]
    h0r = h0r_ref[...]
    h0i = h0i_ref[...]
    hr = bur + (ar * h0r - ai * h0i)
    hi = bui + (ar * h0i + ai * h0r)
    fr_ref[...] = hr
    fi_ref[...] = hi
    kw_out = hr.shape[1] // nblk
    y = _ssm_y(hr, hi, cdr_ref, cdi_ref, nblk, kw_out) + d_ref[...] * u
    z_ref[...] = _gelu_tanh(y).astype(BF16)


def _ssm_step(u, h0r, h0i, disc):
    t, d = u.shape
    bdr, bdi, cdr, cdi, dvec, pr, pi, _, _ = disc
    gn = pr.shape[1]
    return pl.pallas_call(
        functools.partial(_ssm_step_kernel, nblk=bdr.shape[0]),
        out_shape=[jax.ShapeDtypeStruct((t, d), BF16),
                   jax.ShapeDtypeStruct((t, gn), F32), jax.ShapeDtypeStruct((t, gn), F32)],
        name="ssm_step",
        compiler_params=pltpu.CompilerParams(vmem_limit_bytes=V7X_VMEM_LIMIT_BYTES),
    )(u, h0r, h0i, bdr, bdi, cdr, cdi, dvec, pr[0:1], pi[0:1])


def _peer_scores_kernel(q_ref, key_ref, s_ref, *, nkh, nk, dq):
    q = q_ref[...].astype(BF16)
    for c in range(nkh):
        s_ref[c] = _dot_nt(key_ref[c], q[:, c * dq:(c + 1) * dq])


def _peer_scores(q, keys):
    t, n = q.shape
    nkh, nk, dq = keys.shape
    tm = _row_tile(t, 512)
    return pl.pallas_call(
        functools.partial(_peer_scores_kernel, nkh=nkh, nk=nk, dq=dq),
        out_shape=jax.ShapeDtypeStruct((nkh, nk, t), F32),
        grid=(t // tm,),
        in_specs=[pl.BlockSpec((tm, n), lambda i: (i, 0)),
                  pl.BlockSpec((nkh, nk, dq), lambda i: (0, 0, 0))],
        out_specs=pl.BlockSpec((nkh, nk, tm), lambda i: (0, 0, i)),
        compiler_params=_cparams("parallel"),
        name="peer_scores",
    )(q, keys)


def _stable_topk(s, k, idx):
    n = s.shape[0]
    cur = s
    rank = jnp.full(s.shape, float(k), F32)
    vals = []
    for a in range(k):
        m = jnp.max(cur, axis=0, keepdims=True)
        first = jnp.min(jnp.where(cur == m, idx, float(n)), axis=0, keepdims=True)
        hit = idx == first
        rank = jnp.where(hit, float(a), rank)
        cur = jnp.where(hit, NEG_INF, cur)
        vals.append(m)
    return vals, rank


def _peer_select_kernel(s_ref, r2_ref, e2_ref, n1_ref, e1_ref, *, heads, topk):
    nk = s_ref.shape[1]
    lanes = s_ref.shape[2]
    idx = lax.broadcasted_iota(jnp.int32, (nk, lanes), 0).astype(F32)
    cidx = lax.broadcasted_iota(jnp.int32, (topk * topk, lanes), 0).astype(F32)
    ridx = lax.broadcasted_iota(jnp.int32, (topk, lanes), 0).astype(F32)

    def head(h, carry):
        s1 = s_ref[2 * h]
        s2 = s_ref[2 * h + 1]
        v1, rank1 = _stable_topk(s1, topk, idx)
        v2, rank2 = _stable_topk(s2, topk, idx)
        v2all = jnp.zeros((topk, lanes), F32)
        for b in range(topk):
            v2all = jnp.where(ridx == float(b), v2[b], v2all)
        cand = jnp.concatenate([v1[a] + v2all for a in range(topk)], axis=0)
        cur = cand
        tops = []
        for _ in range(topk):
            m = jnp.max(cur, axis=0, keepdims=True)
            first = jnp.min(jnp.where(cur == m, cidx, float(topk * topk)), axis=0, keepdims=True)
            cur = jnp.where(cidx == first, NEG_INF, cur)
            tops.append(m)
        taken = jnp.where(cur == NEG_INF, 1.0, 0.0)
        z = jnp.ones_like(tops[0])
        for m in tops[1:]:
            z = z + jnp.exp(m - tops[0])
        n1 = jnp.zeros((nk, lanes), F32)
        for a in range(topk):
            n_a = jnp.sum(taken[a * topk:(a + 1) * topk, :], axis=0, keepdims=True)
            n1 = jnp.where(rank1 == float(a), n_a, n1)
        r2_ref[h] = rank2
        e2_ref[h] = jnp.exp(s2 - v2[0])
        n1_ref[h] = n1
        e1_ref[h] = jnp.exp(s1 - v1[0]) / z
        return carry

    lax.fori_loop(0, heads, head, 0)


def _peer_select(scores, heads, topk):
    nkh, nk, t = scores.shape
    tl = _row_tile(t, V7X_LANES)
    shp = jax.ShapeDtypeStruct((heads, nk, t), F32)
    ospec = pl.BlockSpec((heads, nk, tl), lambda i: (0, 0, i))
    return pl.pallas_call(
        functools.partial(_peer_select_kernel, heads=heads, topk=topk),
        out_shape=[shp, shp, shp, shp],
        grid=(t // tl,),
        in_specs=[pl.BlockSpec((nkh, nk, tl), lambda i: (0, 0, i))],
        out_specs=[ospec, ospec, ospec, ospec],
        compiler_params=_cparams("parallel"),
        name="peer_select",
    )(scores)


def _peer_dense_kernel(h_ref, u_ref, vt_ref, r2_ref, e2_ref, n1_ref, e1_ref, x_ref, gate_ref, o_ref,
                       at_scr, wt_scr, acc_scr, *, heads, nk, i1_per_step):
    j = pl.program_id(1)

    @pl.when(j == 0)
    def _():
        acc_scr[...] = jnp.zeros_like(acc_scr)

    at_scr[...] = _dot_nt(u_ref[...], h_ref[...])

    def slab(r, carry):
        rows = pl.ds(pl.multiple_of(r * nk, nk), nk)
        g = None
        for h in range(heads):
            n = n1_ref[h, pl.ds(r, 1), :]
            e1 = e1_ref[h, pl.ds(r, 1), :]
            term = jnp.where(r2_ref[h] < n, e2_ref[h], 0.0) * e1
            g = term if g is None else g + term
        wt_scr[rows, :] = (_gelu_tanh(at_scr[rows, :]) * g).astype(BF16)
        return carry

    lax.fori_loop(0, i1_per_step, slab, 0)
    acc_scr[...] += _dot(vt_ref[...], wt_scr[...])

    @pl.when(j == pl.num_programs(1) - 1)
    def _():
        o_ref[...] = x_ref[...] + gate_ref[0] * jnp.transpose(acc_scr[...])


def _peer_dense(h, u, vt, r2, e2, n1, e1, x, gate, seq):
    t, d = x.shape
    ne = u.shape[0]
    heads, nk, _ = r2.shape
    i1_per_step = V7X_SUBLANES
    te = i1_per_step * nk
    tn = _row_tile(min(t, seq) if seq > 1 else t, 512)
    tok = pl.BlockSpec((heads, nk, tn), lambda i, j: (0, 0, i))
    per_i1 = pl.BlockSpec((heads, i1_per_step, tn), lambda i, j: (0, j, i))
    return pl.pallas_call(
        functools.partial(_peer_dense_kernel, heads=heads, nk=nk, i1_per_step=i1_per_step),
        out_shape=jax.ShapeDtypeStruct((t, d), F32),
        grid=(t // tn, ne // te),
        in_specs=[
            pl.BlockSpec((tn, d), lambda i, j: (i, 0)),
            pl.BlockSpec((te, d), lambda i, j: (j, 0)),
            pl.BlockSpec((d, te), lambda i, j: (0, j)),
            tok, tok, per_i1, per_i1,
            pl.BlockSpec((tn, d), lambda i, j: (i, 0)),
            _mod_specs(tn, d, seq),
        ],
        out_specs=pl.BlockSpec((tn, d), lambda i, j: (i, 0)),
        scratch_shapes=[pltpu.VMEM((te, tn), F32), pltpu.VMEM((te, tn), BF16), pltpu.VMEM((d, tn), F32)],
        compiler_params=_cparams("parallel", "arbitrary"),
        name="peer_dense",
    )(h, u, vt, r2, e2, n1, e1, x, gate)


def _rmsnorm_kernel(x_ref, g_ref, o_ref):
    x = x_ref[...]
    o_ref[...] = (x * lax.rsqrt(jnp.mean(x * x, axis=-1, keepdims=True) + EPS)) * g_ref[...]


def _rmsnorm(x, g):
    t, d = x.shape
    tm = _row_tile(t, 1024)
    return pl.pallas_call(
        _rmsnorm_kernel,
        out_shape=jax.ShapeDtypeStruct((t, d), F32),
        grid=(t // tm,),
        in_specs=[pl.BlockSpec((tm, d), lambda i: (i, 0)), pl.BlockSpec((1, d), lambda i: (0, 0))],
        out_specs=pl.BlockSpec((tm, d), lambda i: (i, 0)),
        compiler_params=_cparams("parallel"),
        name="final_rmsnorm",
    )(x, g)


def _mods(mod, lo, hi, seq):
    d = mod.shape[1] // 6
    rows = mod[lo:hi]
    parts = [rows[:, k * d:(k + 1) * d] for k in range(6)]
    if seq == 1:
        return [p[None, :, :] for p in parts]
    return [p[:, None, :] for p in parts]


def kernel(x_prompt, x_sample, c_prompt, c_sample, state_ret, state_ssm_re, state_ssm_im, norm_g, final_g, w_ada, b_ada, ret_w_in, ret_w_out, ssm_w_in, ssm_lam_re, ssm_lam_im, ssm_log_dt, ssm_b_re, ssm_b_im, ssm_c_re, ssm_c_im, ssm_d, ssm_w_glu, peer_w_q, peer_key1, peer_key2, peer_u, peer_v):
    batch, seq, d = x_prompt.shape
    dbatch, dseq, _ = x_sample.shape
    assert dseq == 1 and seq % RET_CHUNK == 0
    depth = w_ada.shape[0]
    heads = peer_key1.shape[1]

    c_all = jnp.concatenate([c_prompt, c_sample], axis=0)
    pad = (-c_all.shape[0]) % V7X_SUBLANES
    c_all = jnp.pad(c_all, ((0, pad), (0, 0)))
    mod = _ada(c_all, w_ada.astype(BF16), b_ada)

    groups = [
        dict(x=x_prompt.reshape(batch * seq, d), lo=0, hi=batch, seq=seq, nb=batch),
        dict(x=x_sample.reshape(dbatch, d), lo=batch, hi=batch + dbatch, seq=1, nb=dbatch),
    ]
    outs = [dict(ret=[], re=[], im=[]) for _ in groups]

    for i in range(depth):
        jm = i // N_MIXERS
        is_ret = i % N_MIXERS == 0
        w_q = peer_w_q[i].astype(BF16)
        keys = jnp.stack([peer_key1[i], peer_key2[i]], axis=1).reshape(2 * heads, PEER_NKEYS, -1).astype(BF16)
        u_tab = peer_u[i].astype(BF16)
        vt_tab = jnp.transpose(peer_v[i]).astype(BF16)
        g_mix = norm_g[i, 0].reshape(1, d)
        g_peer = norm_g[i, 1].reshape(1, d)
        if is_ret:
            w_in = ret_w_in[jm].astype(BF16)
            w_out = ret_w_out[jm].astype(BF16)
        else:
            w_in = ssm_w_in[jm].astype(BF16)
            w_out = ssm_w_glu[jm].astype(BF16)
            a_re, a_im, bbr, bbi = _ssm_discretize(ssm_lam_re[jm], ssm_lam_im[jm], ssm_log_dt[jm],
                                                   ssm_b_re[jm], ssm_b_im[jm])
            per = V7X_MXU_DIM // SSM_GROUP
            pr, pi, sr, si = _complex_pow_tables(a_re, a_im, V7X_SUBLANES)
            disc = (_block_diag(bbr, per).astype(BF16), _block_diag(bbi, per).astype(BF16),
                    _block_diag(jnp.transpose(ssm_c_re[jm], (0, 2, 1)), per).astype(BF16),
                    _block_diag(jnp.transpose(ssm_c_im[jm], (0, 2, 1)), per).astype(BF16),
                    ssm_d[jm].reshape(1, d), pr, pi, sr, si)

        for gi, grp in enumerate(groups):
            x = grp["x"]
            sq = grp["seq"]
            sh1, sc1, g1, sh2, sc2, g2 = _mods(mod[i], grp["lo"], grp["hi"], sq)
            proj = _nm_matmul(x, g_mix, sh1, sc1, w_in, sq)
            if is_ret:
                if sq > 1:
                    y, s = _retention_prompt(proj, grp["nb"], sq, d)
                else:
                    y, s = _retention_step(proj, state_ret[jm], d)
                outs[gi]["ret"].append(s)
                x = _matmul_res(y, w_out, x, g1, sq, glu=False)
            else:
                if sq > 1:
                    z, fr, fi = _ssm_prompt(proj, disc, grp["nb"], sq)
                else:
                    z, fr, fi = _ssm_step(proj, state_ssm_re[jm].reshape(dbatch, -1),
                                          state_ssm_im[jm].reshape(dbatch, -1), disc)
                outs[gi]["re"].append(fr.reshape(grp["nb"], -1, SSM_STATE))
                outs[gi]["im"].append(fi.reshape(grp["nb"], -1, SSM_STATE))
                x = _matmul_res(z, w_out, x, g1, sq, glu=True)
            q, h2 = _nm_matmul(x, g_peer, sh2, sc2, w_q, sq, emit_h=True)
            scores = _peer_scores(q, keys)
            r2, e2, n1, e1 = _peer_select(scores, heads, PEER_TOPK)
            x = _peer_dense(h2, u_tab, vt_tab, r2, e2, n1, e1, x, g2, sq)
            grp["x"] = x

    fg = final_g.reshape(1, d)
    y_prompt = _rmsnorm(groups[0]["x"], fg).reshape(batch, seq, d)
    y_sample = _rmsnorm(groups[1]["x"], fg).reshape(dbatch, 1, d)
    return (y_prompt, y_sample,
            jnp.stack(outs[0]["ret"]), jnp.stack(outs[1]["ret"]),
            jnp.stack(outs[0]["re"]), jnp.stack(outs[0]["im"]),
            jnp.stack(outs[1]["re"]), jnp.stack(outs[1]["im"]))
```

```python
import functools
import math

import jax
import jax.numpy as jnp
import numpy as np
from jax import lax
from jax.experimental import pallas as pl
from jax.experimental.pallas import tpu as pltpu

F32 = jnp.float32
BF16 = jnp.bfloat16

EPS = 1e-6
ROPE_BASE = 10000.0
PAST_LEN = 16384
RET_HEADS = 4
RET_CHUNK = 128
SSM_GROUP = 16
SSM_STATE = 64
PEER_HEADS = 8
PEER_NKEYS = 128
PEER_TOPK = 16
N_MIXERS = 2

V7X_LANES = 128
V7X_SUBLANES = 8
V7X_MXU_DIM = 256
V7X_VMEM_LIMIT_BYTES = 56 * 1024 * 1024

NEG_INF = float("-inf")


def _cparams(*sem):
    return pltpu.CompilerParams(dimension_semantics=sem, vmem_limit_bytes=V7X_VMEM_LIMIT_BYTES)


def _dot(a, b):
    return jnp.dot(a, b, preferred_element_type=F32)


def _dot_nt(a, b):
    return lax.dot_general(a, b, (((1,), (1,)), ((), ())), preferred_element_type=F32)


def _dot_tn(a, b):
    return lax.dot_general(a, b, (((0,), (0,)), ((), ())), preferred_element_type=F32)


def _sigmoid(x):
    return 1.0 / (1.0 + jnp.exp(-x))


def _gelu_tanh(x):
    c = math.sqrt(2.0 / math.pi)
    hx = 0.5 * x
    return hx + hx * jnp.tanh(x * (c + (c * 0.044715) * (x * x)))


def _row_tile(t, pref):
    if t <= pref:
        return t
    tile = pref
    while t % tile:
        tile //= 2
    return tile


def _ada_kernel(c_ref, w_ref, b_ref, o_ref):
    c = c_ref[...]
    sc = (c * _sigmoid(c)).astype(BF16)
    o_ref[0] = _dot(sc, w_ref[0]) + b_ref[0]


def _ada(c_all, w_ada, b_ada):
    m, d = c_all.shape
    depth, _, n = w_ada.shape
    tn = 1024
    return pl.pallas_call(
        _ada_kernel,
        out_shape=jax.ShapeDtypeStruct((depth, m, n), F32),
        grid=(depth, n // tn),
        in_specs=[
            pl.BlockSpec((m, d), lambda l, j: (0, 0)),
            pl.BlockSpec((1, d, tn), lambda l, j: (l, 0, j)),
            pl.BlockSpec((1, 1, tn), lambda l, j: (l, 0, j)),
        ],
        out_specs=pl.BlockSpec((1, m, tn), lambda l, j: (l, 0, j)),
        compiler_params=_cparams("parallel", "parallel"),
        name="ada_mod",
    )(c_all, w_ada, b_ada.reshape(depth, 1, n))


def _norm_mod(x, g, sh, sc):
    y = x * lax.rsqrt(jnp.mean(x * x, axis=-1, keepdims=True) + EPS)
    return (y * g) * (1.0 + sc) + sh


def _nm_matmul_kernel(x_ref, g_ref, sh_ref, sc_ref, w_ref, o_ref, *rest, emit_h):
    if emit_h:
        h_ref, h_scr = rest
    else:
        (h_scr,) = rest
    j = pl.program_id(1)

    @pl.when(j == 0)
    def _():
        h = _norm_mod(x_ref[...], g_ref[...], sh_ref[0], sc_ref[0])
        h_scr[...] = h.astype(BF16)
        if emit_h:
            h_ref[...] = jnp.transpose(h).astype(BF16)

    o_ref[...] = _dot(h_scr[...], w_ref[...])


def _mod_specs(tm, d, seq):
    if seq == 1:
        return pl.BlockSpec((1, tm, d), lambda i, *_: (0, i, 0))
    per = seq // tm
    return pl.BlockSpec((1, 1, d), lambda i, *_: (i // per, 0, 0))


def _nm_matmul(x, g, sh, sc, w, seq, emit_h=False, tm_pref=512, tn_pref=1024):
    t, d = x.shape
    n = w.shape[1]
    tm = _row_tile(min(t, seq) if seq > 1 else t, tm_pref)
    tn = min(n, tn_pref)
    out_shape = [jax.ShapeDtypeStruct((t, n), F32)]
    out_specs = [pl.BlockSpec((tm, tn), lambda i, j: (i, j))]
    if emit_h:
        out_shape.append(jax.ShapeDtypeStruct((d, t), BF16))
        out_specs.append(pl.BlockSpec((d, tm), lambda i, j: (0, i)))
    res = pl.pallas_call(
        functools.partial(_nm_matmul_kernel, emit_h=emit_h),
        out_shape=out_shape,
        grid=(t // tm, n // tn),
        in_specs=[
            pl.BlockSpec((tm, d), lambda i, j: (i, 0)),
            pl.BlockSpec((1, d), lambda i, j: (0, 0)),
            _mod_specs(tm, d, seq),
            _mod_specs(tm, d, seq),
            pl.BlockSpec((d, tn), lambda i, j: (0, j)),
        ],
        out_specs=out_specs,
        scratch_shapes=[pltpu.VMEM((tm, d), BF16)],
        compiler_params=_cparams("parallel", "arbitrary"),
        name="norm_mod_matmul",
    )(x, g, sh, sc, w)
    return res if emit_h else res[0]


def _matmul_res_kernel(a_ref, w_ref, x_ref, gate_ref, o_ref, *, glu):
    r = _dot(a_ref[...], w_ref[...])
    if glu:
        half = r.shape[1] // 2
        r = r[:, :half] * _sigmoid(r[:, half:])
    o_ref[...] = x_ref[...] + gate_ref[0] * r


def _matmul_res(a, w, x, gate, seq, glu, tm_pref=512):
    t, k = a.shape
    n = w.shape[1]
    d = x.shape[1]
    tm = _row_tile(min(t, seq) if seq > 1 else t, tm_pref)
    return pl.pallas_call(
        functools.partial(_matmul_res_kernel, glu=glu),
        out_shape=jax.ShapeDtypeStruct((t, d), F32),
        grid=(t // tm,),
        in_specs=[
            pl.BlockSpec((tm, k), lambda i: (i, 0)),
            pl.BlockSpec((k, n), lambda i: (0, 0)),
            pl.BlockSpec((tm, d), lambda i: (i, 0)),
            _mod_specs(tm, d, seq),
        ],
        out_specs=pl.BlockSpec((tm, d), lambda i: (i, 0)),
        compiler_params=_cparams("parallel"),
        name="matmul_residual",
    )(a, w, x, gate)


def _rotary(x, cos, sin):
    half = x.shape[-1] // 2
    x1 = x[:, :half]
    x2 = x[:, half:]
    return jnp.concatenate([x1 * cos - x2 * sin, x2 * cos + x1 * sin], axis=-1)


def _group_norm_gate(o, g):
    mu = jnp.mean(o, axis=-1, keepdims=True)
    var = jnp.mean(jnp.square(o - mu), axis=-1, keepdims=True)
    on = (o - mu) * lax.rsqrt(var + EPS)
    return (g * _sigmoid(g)) * on


def _ret_chunk_kernel(q_ref, k_ref, v_ref, g_ref, cos_ref, sin_ref, dmask_ref, cross_ref, tail_ref,
                      cdec_ref, o_ref, sfin_ref, s_scr, *, n_sub, chunk, k_scale):
    c = pl.program_id(2)

    @pl.when(c == 0)
    def _():
        s_scr[...] = jnp.zeros_like(s_scr)

    dmask = dmask_ref[0]
    cross = cross_ref[0]
    tail = tail_ref[0]
    cdec = cdec_ref[0]
    for j in range(n_sub):
        rows = pl.ds(j * chunk, chunk)
        cos = cos_ref[rows, :]
        sin = sin_ref[rows, :]
        qc = _rotary(q_ref[rows, :], cos, sin)
        kc = _rotary(k_ref[rows, :], cos, sin) * k_scale
        vb = v_ref[rows, :].astype(BF16)
        qb = qc.astype(BF16)
        att = _dot_nt(qb, kc.astype(BF16)) * dmask
        s = s_scr[...]
        o = _dot(att.astype(BF16), vb) + _dot(qb, s.astype(BF16)) * cross
        s_scr[...] = s * cdec + _dot_tn((kc * tail).astype(BF16), vb)
        o_ref[rows, :] = _group_norm_gate(o, g_ref[rows, :]).astype(BF16)

    @pl.when(c == pl.num_programs(2) - 1)
    def _():
        sfin_ref[0, 0] = s_scr[...]


def _ret_tables(heads, chunk):
    lg = jnp.log1p(-jnp.exp2(-5.0 - jnp.arange(heads, dtype=F32)))
    i = jnp.arange(chunk, dtype=F32)
    diff = i[:, None] - i[None, :]
    dmask = jnp.where(diff >= 0, jnp.exp(lg[:, None, None] * jnp.maximum(diff, 0.0)), 0.0)
    cross = jnp.exp(lg[:, None] * (i[None, :] + 1.0))[:, :, None]
    tail = jnp.exp(lg[:, None] * (chunk - 1.0 - i[None, :]))[:, :, None]
    cdec = jnp.exp(lg * chunk)
    return dmask, cross, tail, cdec


def _rope_tables(pos, half):
    inv = ROPE_BASE ** (-jnp.arange(half, dtype=F32) / half)
    ang = pos.astype(F32)[:, None] * inv[None, :]
    return jnp.cos(ang), jnp.sin(ang)


def _retention_prompt(proj, batch, seq, d):
    heads = RET_HEADS
    dk = d // heads
    dv = 2 * dk
    chunk = RET_CHUNK
    rows = _row_tile(seq, 4 * chunk)
    n_sub = rows // chunk
    nblk = seq // rows
    cos, sin = _rope_tables(jnp.arange(seq, dtype=jnp.int32), dk // 2)
    dmask, cross, tail, cdec = _ret_tables(heads, chunk)
    cdec = jnp.broadcast_to(cdec[:, None, None], (heads, 1, dv))
    kq = heads
    vq = (2 * heads * dk) // dv
    gq = vq + heads
    o, sfin = pl.pallas_call(
        functools.partial(_ret_chunk_kernel, n_sub=n_sub, chunk=chunk, k_scale=dk ** -0.5),
        out_shape=[jax.ShapeDtypeStruct((batch * seq, heads * dv), BF16),
                   jax.ShapeDtypeStruct((batch, heads, dk, dv), F32)],
        grid=(batch, heads, nblk),
        in_specs=[
            pl.BlockSpec((rows, dk), lambda b, h, c: (b * nblk + c, h)),
            pl.BlockSpec((rows, dk), lambda b, h, c: (b * nblk + c, kq + h)),
            pl.BlockSpec((rows, dv), lambda b, h, c: (b * nblk + c, vq + h)),
            pl.BlockSpec((rows, dv), lambda b, h, c: (b * nblk + c, gq + h)),
            pl.BlockSpec((rows, dk // 2), lambda b, h, c: (c, 0)),
            pl.BlockSpec((rows, dk // 2), lambda b, h, c: (c, 0)),
            pl.BlockSpec((1, chunk, chunk), lambda b, h, c: (h, 0, 0)),
            pl.BlockSpec((1, chunk, 1), lambda b, h, c: (h, 0, 0)),
            pl.BlockSpec((1, chunk, 1), lambda b, h, c: (h, 0, 0)),
            pl.BlockSpec((1, 1, dv), lambda b, h, c: (h, 0, 0)),
        ],
        out_specs=[
            pl.BlockSpec((rows, dv), lambda b, h, c: (b * nblk + c, h)),
            pl.BlockSpec((1, 1, dk, dv), lambda b, h, c: (b, h, 0, 0)),
        ],
        scratch_shapes=[pltpu.VMEM((dk, dv), F32)],
        compiler_params=_cparams("parallel", "parallel", "arbitrary"),
        name="retention_chunks",
    )(proj, proj, proj, proj, cos, sin, dmask, cross, tail, cdec)
    return o, sfin


def _ret_step_kernel(q_ref, k_ref, v_ref, g_ref, cos_ref, sin_ref, gam_ref, s0_ref, o_ref, s_ref, *,
                     nb, k_scale):
    cos = cos_ref[...]
    sin = sin_ref[...]
    gam = gam_ref[0]
    qc = _rotary(q_ref[...], cos, sin)
    kc = _rotary(k_ref[...], cos, sin) * k_scale
    qr = qc.astype(BF16).astype(F32)
    kr = kc.astype(BF16).astype(F32)
    vr = v_ref[...].astype(BF16).astype(F32)
    att = jnp.sum(qr * kr, axis=-1, keepdims=True)
    att = att.astype(BF16).astype(F32)
    dk = qc.shape[1]
    eye = (lax.broadcasted_iota(jnp.int32, (dk, dk), 0) == lax.broadcasted_iota(jnp.int32, (dk, dk), 1))
    rows8 = lax.broadcasted_iota(jnp.int32, (V7X_SUBLANES, dk), 0)
    o = jnp.zeros(vr.shape, F32)
    brow = lax.broadcasted_iota(jnp.int32, vr.shape, 0)
    for b in range(nb):
        s0 = s0_ref[b, 0]
        qrow = qc[b:b + 1, :]
        q8 = jnp.where(rows8 == 0, qrow, 0.0).astype(BF16)
        qs = _dot(q8, s0.astype(BF16))[0:1, :]
        o = jnp.where(brow == b, att[b:b + 1, :] * vr[b:b + 1, :] + qs * gam, o)
        kcol = jnp.sum(jnp.where(eye, kr[b:b + 1, :], 0.0), axis=1, keepdims=True)
        s_ref[b, 0] = s0 * gam + kcol * vr[b:b + 1, :]
    o_ref[...] = _group_norm_gate(o, g_ref[...]).astype(BF16)


def _retention_step(proj, state, d):
    t = proj.shape[0]
    heads = RET_HEADS
    dk = d // heads
    dv = 2 * dk
    nb = V7X_SUBLANES
    cos, sin = _rope_tables(PAST_LEN + jnp.arange(1, dtype=jnp.int32), dk // 2)
    lg = jnp.log1p(-jnp.exp2(-5.0 - jnp.arange(heads, dtype=F32)))
    gam = jnp.broadcast_to(jnp.exp(lg)[:, None, None], (heads, 1, dv))
    kq = heads
    vq = (2 * heads * dk) // dv
    gq = vq + heads
    o, s = pl.pallas_call(
        functools.partial(_ret_step_kernel, nb=nb, k_scale=dk ** -0.5),
        out_shape=[jax.ShapeDtypeStruct((t, heads * dv), BF16),
                   jax.ShapeDtypeStruct(state.shape, F32)],
        grid=(t // nb, heads),
        in_specs=[
            pl.BlockSpec((nb, dk), lambda i, h: (i, h)),
            pl.BlockSpec((nb, dk), lambda i, h: (i, kq + h)),
            pl.BlockSpec((nb, dv), lambda i, h: (i, vq + h)),
            pl.BlockSpec((nb, dv), lambda i, h: (i, gq + h)),
            pl.BlockSpec((1, dk // 2), lambda i, h: (0, 0)),
            pl.BlockSpec((1, dk // 2), lambda i, h: (0, 0)),
            pl.BlockSpec((1, 1, dv), lambda i, h: (h, 0, 0)),
            pl.BlockSpec((nb, 1, dk, dv), lambda i, h: (i, h, 0, 0)),
        ],
        out_specs=[
            pl.BlockSpec((nb, dv), lambda i, h: (i, h)),
            pl.BlockSpec((nb, 1, dk, dv), lambda i, h: (i, h, 0, 0)),
        ],
        compiler_params=_cparams("parallel", "parallel"),
        name="retention_step",
    )(proj, proj, proj, proj, cos, sin, gam, state)
    return o, s


def _ssm_disc_kernel(lr_ref, li_ref, ldt_ref, br_ref, bi_ref, are_ref, aim_ref, bbr_ref, bbi_ref):
    lr = lr_ref[...]
    li = li_ref[...]
    dt = jnp.exp(ldt_ref[...])
    mag = jnp.exp(lr * dt)
    a_re = mag * jnp.cos(li * dt)
    a_im = mag * jnp.sin(li * dt)
    den = lr * lr + li * li
    nr = a_re - 1.0
    c_re = (nr * lr + a_im * li) / den
    c_im = (a_im * lr - nr * li) / den
    are_ref[...] = a_re
    aim_ref[...] = a_im
    br = br_ref[...]
    bi = bi_ref[...]
    bbr_ref[...] = c_re * br - c_im * bi
    bbi_ref[...] = c_re * bi + c_im * br


def _ssm_discretize(lam_re, lam_im, log_dt, b_re, b_im):
    g, n = lam_re.shape
    p = b_re.shape[2]
    brt = jnp.transpose(b_re, (0, 2, 1))
    bit = jnp.transpose(b_im, (0, 2, 1))
    a_re, a_im, bbr, bbi = pl.pallas_call(
        _ssm_disc_kernel,
        out_shape=[jax.ShapeDtypeStruct((g, 1, n), F32), jax.ShapeDtypeStruct((g, 1, n), F32),
                   jax.ShapeDtypeStruct((g, p, n), F32), jax.ShapeDtypeStruct((g, p, n), F32)],
        name="ssm_discretize",
    )(lam_re.reshape(g, 1, n), lam_im.reshape(g, 1, n), log_dt.reshape(g, 1, 1), brt, bit)
    return a_re.reshape(g, n), a_im.reshape(g, n), bbr, bbi


def _block_diag(w, per):
    g, a, b = w.shape
    w = w.reshape(g // per, per, a, b)
    eye = jnp.eye(per, dtype=w.dtype)
    return jnp.einsum('cgab,gh->cgahb', w, eye).reshape(g // per, per * a, per * b)


def _complex_pow_tables(a_re, a_im, rows):
    are = a_re.reshape(1, -1)
    aim = a_im.reshape(1, -1)
    pr, pi = [are], [aim]
    for _ in range(rows - 1):
        r, i = pr[-1], pi[-1]
        pr.append(r * are - i * aim)
        pi.append(r * aim + i * are)
    steps = [1 << k for k in range(int(math.log2(rows)))]
    sr = jnp.concatenate([pr[s - 1] for s in steps], axis=0)
    si = jnp.concatenate([pi[s - 1] for s in steps], axis=0)
    return jnp.concatenate(pr, axis=0), jnp.concatenate(pi, axis=0), sr, si


def _ssm_bu(ub, bdr_ref, bdi_ref, nblk, kw):
    bur = jnp.concatenate([_dot(ub[:, c * kw:(c + 1) * kw], bdr_ref[c]) for c in range(nblk)], axis=1)
    bui = jnp.concatenate([_dot(ub[:, c * kw:(c + 1) * kw], bdi_ref[c]) for c in range(nblk)], axis=1)
    return bur, bui


def _ssm_y(hr, hi, cdr_ref, cdi_ref, nblk, kw):
    hrb = hr.astype(BF16)
    hib = hi.astype(BF16)
    return jnp.concatenate(
        [_dot(hrb[:, c * kw:(c + 1) * kw], cdr_ref[c]) - _dot(hib[:, c * kw:(c + 1) * kw], cdi_ref[c])
         for c in range(nblk)], axis=1)


def _ssm_scan_kernel(u_ref, bdr_ref, bdi_ref, cdr_ref, cdi_ref, d_ref, pr_ref, pi_ref, sr_ref, si_ref,
                     z_ref, fr_ref, fi_ref, hr_scr, hi_scr, cr_scr, ci_scr, *, nblk, ts):
    sub = V7X_SUBLANES
    step = pl.program_id(1)

    @pl.when(step == 0)
    def _():
        cr_scr[...] = jnp.zeros_like(cr_scr)
        ci_scr[...] = jnp.zeros_like(ci_scr)

    u = u_ref[...]
    kw_in = u.shape[1] // nblk
    bur, bui = _ssm_bu(u.astype(BF16), bdr_ref, bdi_ref, nblk, kw_in)
    hr_scr[...] = bur
    hi_scr[...] = bui

    gn = hr_scr.shape[1]
    row = lax.broadcasted_iota(jnp.int32, (sub, gn), 0)
    shifts = [1 << k for k in range(int(math.log2(sub)))]
    masks = [row >= s for s in shifts]

    def group(r, carry):
        cr, ci = carry
        rows = pl.ds(pl.multiple_of(r * sub, sub), sub)
        xr = hr_scr[rows, :]
        xi = hi_scr[rows, :]
        for k, s in enumerate(shifts):
            ar = sr_ref[k:k + 1, :]
            ai = si_ref[k:k + 1, :]
            yr = jnp.where(masks[k], pltpu.roll(xr, s, 0), 0.0)
            yi = jnp.where(masks[k], pltpu.roll(xi, s, 0), 0.0)
            xr, xi = xr + (ar * yr - ai * yi), xi + (ar * yi + ai * yr)
        pr = pr_ref[...]
        pi = pi_ref[...]
        xr, xi = xr + (pr * cr - pi * ci), xi + (pr * ci + pi * cr)
        hr_scr[rows, :] = xr
        hi_scr[rows, :] = xi
        return xr[sub - 1:sub, :], xi[sub - 1:sub, :]

    cr, ci = lax.fori_loop(0, ts // sub, group, (cr_scr[...], ci_scr[...]))
    cr_scr[...] = cr
    ci_scr[...] = ci

    kw_out = gn // nblk
    y = _ssm_y(hr_scr[...], hi_scr[...], cdr_ref, cdi_ref, nblk, kw_out) + d_ref[...] * u
    z_ref[...] = _gelu_tanh(y).astype(BF16)

    @pl.when(step == pl.num_programs(1) - 1)
    def _():
        fr_ref[0] = cr
        fi_ref[0] = ci


def _ssm_prompt(u, disc, batch, seq):
    t, d = u.shape
    bdr, bdi, cdr, cdi, dvec, pr, pi, sr, si = disc
    nblk = bdr.shape[0]
    gn = pr.shape[1]
    ts = _row_tile(seq, 256)
    nstep = seq // ts
    full = lambda a: pl.BlockSpec(a.shape, lambda b, s: (0,) * a.ndim)
    z, fr, fi = pl.pallas_call(
        functools.partial(_ssm_scan_kernel, nblk=nblk, ts=ts),
        out_shape=[jax.ShapeDtypeStruct((t, d), BF16),
                   jax.ShapeDtypeStruct((batch, 1, gn), F32),
                   jax.ShapeDtypeStruct((batch, 1, gn), F32)],
        grid=(batch, nstep),
        in_specs=[pl.BlockSpec((ts, d), lambda b, s: (b * nstep + s, 0)),
                  full(bdr), full(bdi), full(cdr), full(cdi), full(dvec), full(pr), full(pi), full(sr), full(si)],
        out_specs=[pl.BlockSpec((ts, d), lambda b, s: (b * nstep + s, 0)),
                   pl.BlockSpec((1, 1, gn), lambda b, s: (b, 0, 0)),
                   pl.BlockSpec((1, 1, gn), lambda b, s: (b, 0, 0))],
        scratch_shapes=[pltpu.VMEM((ts, gn), F32), pltpu.VMEM((ts, gn), F32),
                        pltpu.VMEM((1, gn), F32), pltpu.VMEM((1, gn), F32)],
        compiler_params=_cparams("parallel", "arbitrary"),
        name="ssm_scan",
    )(u, bdr, bdi, cdr, cdi, dvec, pr, pi, sr, si)
    return z, fr, fi


def _ssm_step_kernel(u_ref, h0r_ref, h0i_ref, bdr_ref, bdi_ref, cdr_ref, cdi_ref, d_ref, ar_ref, ai_ref,
                     z_ref, fr_ref, fi_ref, *, nblk):
    u = u_ref[...]
    kw_in = u.shape[1] // nblk
    bur, bui = _ssm_bu(u.astype(BF16), bdr_ref, bdi_ref, nblk, kw_in)
    ar = ar_ref[...]
    ai = ai_ref[...]
    h0r = h0r_ref[...]
    h0i = h0i_ref[...]
    hr = bur + (ar * h0r - ai * h0i)
    hi = bui + (ar * h0i + ai * h0r)
    fr_ref[...] = hr
    fi_ref[...] = hi
    kw_out = hr.shape[1] // nblk
    y = _ssm_y(hr, hi, cdr_ref, cdi_ref, nblk, kw_out) + d_ref[...] * u
    z_ref[...] = _gelu_tanh(y).astype(BF16)


def _ssm_step(u, h0r, h0i, disc):
    t, d = u.shape
    bdr, bdi, cdr, cdi, dvec, pr, pi, _, _ = disc
    gn = pr.shape[1]
    return pl.pallas_call(
        functools.partial(_ssm_step_kernel, nblk=bdr.shape[0]),
        out_shape=[jax.ShapeDtypeStruct((t, d), BF16),
                   jax.ShapeDtypeStruct((t, gn), F32), jax.ShapeDtypeStruct((t, gn), F32)],
        name="ssm_step",
        compiler_params=pltpu.CompilerParams(vmem_limit_bytes=V7X_VMEM_LIMIT_BYTES),
    )(u, h0r, h0i, bdr, bdi, cdr, cdi, dvec, pr[0:1], pi[0:1])


def _peer_scores_kernel(q_ref, key_ref, s_ref, *, nkh, nk, dq):
    q = q_ref[...].astype(BF16)
    for c in range(nkh):
        s_ref[c] = _dot_nt(key_ref[c], q[:, c * dq:(c + 1) * dq])


def _peer_scores(q, keys):
    t, n = q.shape
    nkh, nk, dq = keys.shape
    tm = _row_tile(t, 512)
    return pl.pallas_call(
        functools.partial(_peer_scores_kernel, nkh=nkh, nk=nk, dq=dq),
        out_shape=jax.ShapeDtypeStruct((nkh, nk, t), F32),
        grid=(t // tm,),
        in_specs=[pl.BlockSpec((tm, n), lambda i: (i, 0)),
                  pl.BlockSpec((nkh, nk, dq), lambda i: (0, 0, 0))],
        out_specs=pl.BlockSpec((nkh, nk, tm), lambda i: (0, 0, i)),
        compiler_params=_cparams("parallel"),
        name="peer_scores",
    )(q, keys)


def _remove_max_rounds(s, k, idx, want_round):
    cur = s
    rnd = jnp.full(s.shape, float(k), F32) if want_round else None
    vals = []
    for a in range(k):
        m = jnp.max(cur, axis=0, keepdims=True)
        if idx is None:
            hit = cur == m
        else:
            first = jnp.min(jnp.where(cur == m, idx, float(s.shape[0])), axis=0, keepdims=True)
            hit = idx == first
        if want_round:
            rnd = jnp.where(hit, float(a), rnd)
        cur = jnp.where(hit, NEG_INF, cur)
        vals.append(m)
    return vals, rnd, cur


def _rows_from_list(vals, lo, n, ridx):
    out = jnp.zeros(ridx.shape, F32)
    for r in range(n):
        out = jnp.where(ridx == float(r), vals[lo + r], out)
    return out


def _peer_select_one(s1, s2, topk, idx, cidx, valid, ridx_k, ridx_s):
    sub = V7X_SUBLANES
    stair = [topk // (a + 1) for a in range(topk)]
    v1, rank1, cur1 = _remove_max_rounds(s1, topk, idx, True)
    v2, rank2, cur2 = _remove_max_rounds(s2, topk, idx, True)
    v2all = _rows_from_list(v2, 0, topk, ridx_k)
    v2lo = _rows_from_list(v2, 0, sub, ridx_s)
    v1hi = _rows_from_list(v1, sub, topk - sub, ridx_s)
    blocks = [v1[0] + v2all] + [v1[a] + v2lo for a in range(1, sub)] + [v1hi + v2[0]]
    cand = jnp.where(valid, jnp.concatenate(blocks, axis=0), NEG_INF)
    tops, _, curc = _remove_max_rounds(cand, topk, cidx, False)
    taken = jnp.where(jnp.logical_and(curc == NEG_INF, valid), 1.0, 0.0)
    z = jnp.ones_like(tops[0])
    for m in tops[1:]:
        z = z + jnp.exp(m - tops[0])
    n1 = jnp.zeros(s1.shape, F32)
    total = jnp.zeros_like(z)
    for a in range(topk):
        if a == 0:
            n_a = jnp.sum(taken[0:topk, :], axis=0, keepdims=True)
        elif a < sub:
            base = topk + sub * (a - 1)
            n_a = jnp.sum(taken[base:base + sub, :], axis=0, keepdims=True)
        else:
            row = topk + sub * (sub - 1) + (a - sub)
            n_a = taken[row:row + 1, :]
        total = total + n_a
        n1 = jnp.where(rank1 == float(a), n_a, n1)
    if idx is None:
        k = float(topk)
        c1 = jnp.sum(jnp.where(cur1 == NEG_INF, 1.0, 0.0), axis=0, keepdims=True)
        c2 = jnp.sum(jnp.where(cur2 == NEG_INF, 1.0, 0.0), axis=0, keepdims=True)
        bad = jnp.where(jnp.logical_and(jnp.logical_and(c1 == k, c2 == k), total == k), 0.0, 1.0)
    else:
        bad = None
    e2 = jnp.exp(s2 - v2[0])
    e1 = jnp.exp(s1 - v1[0]) / z
    return rank2, e2, n1, e1, bad


def _peer_select_kernel(s_ref, r2_ref, e2_ref, n1_ref, e1_ref, *, heads, topk):
    sub = V7X_SUBLANES
    nk = s_ref.shape[1]
    lanes = s_ref.shape[2]
    ncand = topk + sub * sub
    idx = lax.broadcasted_iota(jnp.int32, (nk, lanes), 0).astype(F32)
    crow = lax.broadcasted_iota(jnp.int32, (ncand, lanes), 0)
    cidx = crow.astype(F32)
    ridx_k = lax.broadcasted_iota(jnp.int32, (topk, lanes), 0).astype(F32)
    ridx_s = lax.broadcasted_iota(jnp.int32, (sub, lanes), 0).astype(F32)
    valid = crow < topk + sub
    for a in range(2, sub):
        base = topk + sub * (a - 1)
        valid = jnp.logical_or(valid, jnp.logical_and(crow >= base, crow < base + topk // (a + 1)))
    valid = jnp.logical_or(valid, crow >= topk + sub * (sub - 1))

    def write(h, res):
        r2_ref[h], e2_ref[h], n1_ref[h], e1_ref[h] = res

    def one_head(h):
        s1 = s_ref[2 * h]
        s2 = s_ref[2 * h + 1]
        *res, bad = _peer_select_one(s1, s2, topk, None, None, valid, ridx_k, ridx_s)
        write(h, res)

        @pl.when(jnp.max(bad) > 0.0)
        def _():
            *res, _ = _peer_select_one(s1, s2, topk, idx, cidx, valid, ridx_k, ridx_s)
            write(h, res)

    def head_pair(hp, carry):
        one_head(2 * hp)
        one_head(2 * hp + 1)
        return carry

    lax.fori_loop(0, heads // 2, head_pair, 0)


def _peer_select(scores, heads, topk):
    nkh, nk, t = scores.shape
    tl = _row_tile(t, V7X_LANES)
    shp = jax.ShapeDtypeStruct((heads, nk, t), F32)
    ospec = pl.BlockSpec((heads, nk, tl), lambda i: (0, 0, i))
    return pl.pallas_call(
        functools.partial(_peer_select_kernel, heads=heads, topk=topk),
        out_shape=[shp, shp, shp, shp],
        grid=(t // tl,),
        in_specs=[pl.BlockSpec((nkh, nk, tl), lambda i: (0, 0, i))],
        out_specs=[ospec, ospec, ospec, ospec],
        compiler_params=_cparams("parallel"),
        name="peer_select",
    )(scores)


def _peer_dense_kernel(ht_ref, u_ref, vt_ref, r2_ref, e2_ref, n1_ref, e1_ref, x_ref, gate_ref, o_ref,
                       at0_scr, at1_scr, wt_scr, acc_scr, *, heads, nk, i1_per_step, nblk):
    j = pl.program_id(1)
    at_scr = (at0_scr, at1_scr)

    def activations(dst):
        dst[...] = _dot(u_ref[...], ht_ref[...])

    def mix(src):
        for r in range(i1_per_step):
            rows = pl.ds(r * nk, nk)
            g = None
            for h in range(heads):
                n = n1_ref[h, r:r + 1, :]
                e1 = e1_ref[h, r:r + 1, :]
                term = jnp.where(r2_ref[h] < n, e2_ref[h], 0.0) * e1
                g = term if g is None else g + term
            wt_scr[rows, :] = (_gelu_tanh(src[rows, :]) * g).astype(BF16)
        acc_scr[...] += _dot(vt_ref[...], wt_scr[...])

    @pl.when(j == 0)
    def _():
        acc_scr[...] = jnp.zeros_like(acc_scr)
        activations(at_scr[0])

    for parity in range(2):
        @pl.when(jnp.logical_and(jnp.logical_and(j > 0, j < nblk), j % 2 == parity))
        def _():
            activations(at_scr[parity])
            mix(at_scr[1 - parity])

    @pl.when(j == nblk)
    def _():
        mix(at_scr[(nblk - 1) % 2])
        o_ref[...] = x_ref[...] + gate_ref[0] * jnp.transpose(acc_scr[...])


def _peer_dense(ht, u, vt, r2, e2, n1, e1, x, gate, seq):
    t, d = x.shape
    ne = u.shape[0]
    heads, nk, _ = r2.shape
    i1_per_step = V7X_SUBLANES
    te = i1_per_step * nk
    nblk = ne // te
    tn = _row_tile(min(t, seq) if seq > 1 else t, 512)
    clamp = lambda b: jnp.clip(b, 0, nblk - 1)
    tok = pl.BlockSpec((heads, nk, tn), lambda i, j: (0, 0, i))
    per_i1 = pl.BlockSpec((heads, i1_per_step, tn), lambda i, j: (0, clamp(j - 1), i))
    return pl.pallas_call(
        functools.partial(_peer_dense_kernel, heads=heads, nk=nk, i1_per_step=i1_per_step, nblk=nblk),
        out_shape=jax.ShapeDtypeStruct((t, d), F32),
        grid=(t // tn, nblk + 1),
        in_specs=[
            pl.BlockSpec((d, tn), lambda i, j: (0, i)),
            pl.BlockSpec((te, d), lambda i, j: (clamp(j), 0)),
            pl.BlockSpec((d, te), lambda i, j: (0, clamp(j - 1))),
            tok, tok, per_i1, per_i1,
            pl.BlockSpec((tn, d), lambda i, j: (i, 0)),
            _mod_specs(tn, d, seq),
        ],
        out_specs=pl.BlockSpec((tn, d), lambda i, j: (i, 0)),
        scratch_shapes=[pltpu.VMEM((te, tn), F32), pltpu.VMEM((te, tn), F32),
                        pltpu.VMEM((te, tn), BF16), pltpu.VMEM((d, tn), F32)],
        compiler_params=_cparams("parallel", "arbitrary"),
        name="peer_dense",
    )(ht, u, vt, r2, e2, n1, e1, x, gate)


def _rmsnorm_kernel(x_ref, g_ref, o_ref):
    x = x_ref[...]
    o_ref[...] = (x * lax.rsqrt(jnp.mean(x * x, axis=-1, keepdims=True) + EPS)) * g_ref[...]


def _rmsnorm(x, g):
    t, d = x.shape
    tm = _row_tile(t, 1024)
    return pl.pallas_call(
        _rmsnorm_kernel,
        out_shape=jax.ShapeDtypeStruct((t, d), F32),
        grid=(t // tm,),
        in_specs=[pl.BlockSpec((tm, d), lambda i: (i, 0)), pl.BlockSpec((1, d), lambda i: (0, 0))],
        out_specs=pl.BlockSpec((tm, d), lambda i: (i, 0)),
        compiler_params=_cparams("parallel"),
        name="final_rmsnorm",
    )(x, g)


def _mods(mod, lo, hi, seq):
    d = mod.shape[1] // 6
    rows = mod[lo:hi]
    parts = [rows[:, k * d:(k + 1) * d] for k in range(6)]
    if seq == 1:
        return [p[None, :, :] for p in parts]
    return [p[:, None, :] for p in parts]


def kernel(x_prompt, x_sample, c_prompt, c_sample, state_ret, state_ssm_re, state_ssm_im, norm_g, final_g, w_ada, b_ada, ret_w_in, ret_w_out, ssm_w_in, ssm_lam_re, ssm_lam_im, ssm_log_dt, ssm_b_re, ssm_b_im, ssm_c_re, ssm_c_im, ssm_d, ssm_w_glu, peer_w_q, peer_key1, peer_key2, peer_u, peer_v):
    batch, seq, d = x_prompt.shape
    dbatch, dseq, _ = x_sample.shape
    assert dseq == 1 and seq % RET_CHUNK == 0
    depth = w_ada.shape[0]
    heads = peer_key1.shape[1]

    c_all = jnp.concatenate([c_prompt, c_sample], axis=0)
    pad = (-c_all.shape[0]) % V7X_SUBLANES
    c_all = jnp.pad(c_all, ((0, pad), (0, 0)))
    mod = _ada(c_all, w_ada.astype(BF16), b_ada)

    groups = [
        dict(x=x_prompt.reshape(batch * seq, d), lo=0, hi=batch, seq=seq, nb=batch),
        dict(x=x_sample.reshape(dbatch, d), lo=batch, hi=batch + dbatch, seq=1, nb=dbatch),
    ]
    outs = [dict(ret=[], re=[], im=[]) for _ in groups]

    for i in range(depth):
        jm = i // N_MIXERS
        is_ret = i % N_MIXERS == 0
        w_q = peer_w_q[i].astype(BF16)
        keys = jnp.stack([peer_key1[i], peer_key2[i]], axis=1).reshape(2 * heads, PEER_NKEYS, -1).astype(BF16)
        u_tab = peer_u[i].astype(BF16)
        vt_tab = jnp.transpose(peer_v[i]).astype(BF16)
        g_mix = norm_g[i, 0].reshape(1, d)
        g_peer = norm_g[i, 1].reshape(1, d)
        if is_ret:
            w_in = ret_w_in[jm].astype(BF16)
            w_out = ret_w_out[jm].astype(BF16)
        else:
            w_in = ssm_w_in[jm].astype(BF16)
            w_out = ssm_w_glu[jm].astype(BF16)
            a_re, a_im, bbr, bbi = _ssm_discretize(ssm_lam_re[jm], ssm_lam_im[jm], ssm_log_dt[jm],
                                                   ssm_b_re[jm], ssm_b_im[jm])
            per = V7X_MXU_DIM // SSM_GROUP
            pr, pi, sr, si = _complex_pow_tables(a_re, a_im, V7X_SUBLANES)
            disc = (_block_diag(bbr, per).astype(BF16), _block_diag(bbi, per).astype(BF16),
                    _block_diag(jnp.transpose(ssm_c_re[jm], (0, 2, 1)), per).astype(BF16),
                    _block_diag(jnp.transpose(ssm_c_im[jm], (0, 2, 1)), per).astype(BF16),
                    ssm_d[jm].reshape(1, d), pr, pi, sr, si)

        for gi, grp in enumerate(groups):
            x = grp["x"]
            sq = grp["seq"]
            sh1, sc1, g1, sh2, sc2, g2 = _mods(mod[i], grp["lo"], grp["hi"], sq)
            proj = _nm_matmul(x, g_mix, sh1, sc1, w_in, sq)
            if is_ret:
                if sq > 1:
                    y, s = _retention_prompt(proj, grp["nb"], sq, d)
                else:
                    y, s = _retention_step(proj, state_ret[jm], d)
                outs[gi]["ret"].append(s)
                x = _matmul_res(y, w_out, x, g1, sq, glu=False)
            else:
                if sq > 1:
                    z, fr, fi = _ssm_prompt(proj, disc, grp["nb"], sq)
                else:
                    z, fr, fi = _ssm_step(proj, state_ssm_re[jm].reshape(dbatch, -1),
                                          state_ssm_im[jm].reshape(dbatch, -1), disc)
                outs[gi]["re"].append(fr.reshape(grp["nb"], -1, SSM_STATE))
                outs[gi]["im"].append(fi.reshape(grp["nb"], -1, SSM_STATE))
                x = _matmul_res(z, w_out, x, g1, sq, glu=True)
            q, h2 = _nm_matmul(x, g_peer, sh2, sc2, w_q, sq, emit_h=True)
            scores = _peer_scores(q, keys)
            r2, e2, n1, e1 = _peer_select(scores, heads, PEER_TOPK)
            x = _peer_dense(h2, u_tab, vt_tab, r2, e2, n1, e1, x, g2, sq)
            grp["x"] = x

    fg = final_g.reshape(1, d)
    y_prompt = _rmsnorm(groups[0]["x"], fg).reshape(batch, seq, d)
    y_sample = _rmsnorm(groups[1]["x"], fg).reshape(dbatch, 1, d)
    return (y_prompt, y_sample,
            jnp.stack(outs[0]["ret"]), jnp.stack(outs[1]["ret"]),
            jnp.stack(outs[0]["re"]), jnp.stack(outs[0]["im"]),
            jnp.stack(outs[1]["re"]), jnp.stack(outs[1]["im"]))
```

```python
import functools
import math

import jax
import jax.numpy as jnp
import numpy as np
from jax import lax
from jax.experimental import pallas as pl
from jax.experimental.pallas import tpu as pltpu

F32 = jnp.float32
BF16 = jnp.bfloat16

EPS = 1e-6
ROPE_BASE = 10000.0
PAST_LEN = 16384
RET_HEADS = 4
RET_CHUNK = 128
SSM_GROUP = 16
SSM_STATE = 64
PEER_HEADS = 8
PEER_NKEYS = 128
PEER_TOPK = 16
N_MIXERS = 2

V7X_LANES = 128
V7X_SUBLANES = 8
V7X_MXU_DIM = 256
V7X_VMEM_LIMIT_BYTES = 56 * 1024 * 1024

NEG_INF = float("-inf")


def _cparams(*sem):
    return pltpu.CompilerParams(dimension_semantics=sem, vmem_limit_bytes=V7X_VMEM_LIMIT_BYTES)


def _dot(a, b):
    return jnp.dot(a, b, preferred_element_type=F32)


def _dot_nt(a, b):
    return lax.dot_general(a, b, (((1,), (1,)), ((), ())), preferred_element_type=F32)


def _dot_tn(a, b):
    return lax.dot_general(a, b, (((0,), (0,)), ((), ())), preferred_element_type=F32)


def _sigmoid(x):
    return 1.0 / (1.0 + jnp.exp(-x))


def _gelu_tanh(x):
    c = math.sqrt(2.0 / math.pi)
    hx = 0.5 * x
    return hx + hx * jnp.tanh(x * (c + (c * 0.044715) * (x * x)))


def _row_tile(t, pref):
    if t <= pref:
        return t
    tile = pref
    while t % tile:
        tile //= 2
    return tile


def _ada_kernel(c_ref, w_ref, b_ref, o_ref):
    c = c_ref[...]
    sc = (c * _sigmoid(c)).astype(BF16)
    o_ref[0] = _dot(sc, w_ref[0]) + b_ref[0]


def _ada(c_all, w_ada, b_ada):
    m, d = c_all.shape
    depth, _, n = w_ada.shape
    tn = 1024
    return pl.pallas_call(
        _ada_kernel,
        out_shape=jax.ShapeDtypeStruct((depth, m, n), F32),
        grid=(depth, n // tn),
        in_specs=[
            pl.BlockSpec((m, d), lambda l, j: (0, 0)),
            pl.BlockSpec((1, d, tn), lambda l, j: (l, 0, j)),
            pl.BlockSpec((1, 1, tn), lambda l, j: (l, 0, j)),
        ],
        out_specs=pl.BlockSpec((1, m, tn), lambda l, j: (l, 0, j)),
        compiler_params=_cparams("parallel", "parallel"),
        name="ada_mod",
    )(c_all, w_ada, b_ada.reshape(depth, 1, n))


def _norm_mod(x, g, sh, sc):
    y = x * lax.rsqrt(jnp.mean(x * x, axis=-1, keepdims=True) + EPS)
    return (y * g) * (1.0 + sc) + sh


def _nm_matmul_kernel(x_ref, g_ref, sh_ref, sc_ref, w_ref, o_ref, *rest, emit_h):
    if emit_h:
        h_ref, h_scr = rest
    else:
        (h_scr,) = rest
    j = pl.program_id(1)

    @pl.when(j == 0)
    def _():
        h = _norm_mod(x_ref[...], g_ref[...], sh_ref[0], sc_ref[0])
        h_scr[...] = h.astype(BF16)
        if emit_h:
            h_ref[...] = jnp.transpose(h).astype(BF16)

    o_ref[...] = _dot(h_scr[...], w_ref[...])


def _mod_specs(tm, d, seq):
    if seq == 1:
        return pl.BlockSpec((1, tm, d), lambda i, *_: (0, i, 0))
    per = seq // tm
    return pl.BlockSpec((1, 1, d), lambda i, *_: (i // per, 0, 0))


def _nm_matmul(x, g, sh, sc, w, seq, emit_h=False, time_major=False, tm_pref=512, tn_pref=1024):
    t, d = x.shape
    n = w.shape[1]
    tm = _row_tile(min(t, seq) if seq > 1 else t, tm_pref)
    tn = min(n, tn_pref)
    if time_major:
        per, nj = seq // tm, n // tn
        out_shape = [jax.ShapeDtypeStruct((seq, (t // seq) * n), F32)]
        out_specs = [pl.BlockSpec((tm, tn), lambda i, j: (i % per, (i // per) * nj + j))]
    else:
        out_shape = [jax.ShapeDtypeStruct((t, n), F32)]
        out_specs = [pl.BlockSpec((tm, tn), lambda i, j: (i, j))]
    if emit_h:
        out_shape.append(jax.ShapeDtypeStruct((d, t), BF16))
        out_specs.append(pl.BlockSpec((d, tm), lambda i, j: (0, i)))
    res = pl.pallas_call(
        functools.partial(_nm_matmul_kernel, emit_h=emit_h),
        out_shape=out_shape,
        grid=(t // tm, n // tn),
        in_specs=[
            pl.BlockSpec((tm, d), lambda i, j: (i, 0)),
            pl.BlockSpec((1, d), lambda i, j: (0, 0)),
            _mod_specs(tm, d, seq),
            _mod_specs(tm, d, seq),
            pl.BlockSpec((d, tn), lambda i, j: (0, j)),
        ],
        out_specs=out_specs,
        scratch_shapes=[pltpu.VMEM((tm, d), BF16)],
        compiler_params=_cparams("parallel", "arbitrary"),
        name="norm_mod_matmul",
    )(x, g, sh, sc, w)
    res = list(res)
    if time_major:
        res[0] = res[0].reshape(t, n)
    return res if emit_h else res[0]


def _matmul_res_kernel(a_ref, w_ref, x_ref, gate_ref, o_ref, *, glu):
    r = _dot(a_ref[...], w_ref[...])
    if glu:
        half = r.shape[1] // 2
        r = r[:, :half] * _sigmoid(r[:, half:])
    o_ref[...] = x_ref[...] + gate_ref[0] * r


def _matmul_res(a, w, x, gate, seq, glu, a_time_major=False, tm_pref=512):
    t, k = a.shape
    n = w.shape[1]
    d = x.shape[1]
    tm = _row_tile(min(t, seq) if seq > 1 else t, tm_pref)
    if a_time_major:
        per = seq // tm
        a = a.reshape(seq, (t // seq) * k)
        a_spec = pl.BlockSpec((tm, k), lambda i: (i % per, i // per))
    else:
        a_spec = pl.BlockSpec((tm, k), lambda i: (i, 0))
    return pl.pallas_call(
        functools.partial(_matmul_res_kernel, glu=glu),
        out_shape=jax.ShapeDtypeStruct((t, d), F32),
        grid=(t // tm,),
        in_specs=[
            a_spec,
            pl.BlockSpec((k, n), lambda i: (0, 0)),
            pl.BlockSpec((tm, d), lambda i: (i, 0)),
            _mod_specs(tm, d, seq),
        ],
        out_specs=pl.BlockSpec((tm, d), lambda i: (i, 0)),
        compiler_params=_cparams("parallel"),
        name="matmul_residual",
    )(a, w, x, gate)


def _rotary(x, cos, sin):
    half = x.shape[-1] // 2
    x1 = x[:, :half]
    x2 = x[:, half:]
    return jnp.concatenate([x1 * cos - x2 * sin, x2 * cos + x1 * sin], axis=-1)


def _group_norm_gate(o, g):
    mu = jnp.mean(o, axis=-1, keepdims=True)
    var = jnp.mean(jnp.square(o - mu), axis=-1, keepdims=True)
    on = (o - mu) * lax.rsqrt(var + EPS)
    return (g * _sigmoid(g)) * on


def _ret_chunk_kernel(q_ref, k_ref, v_ref, g_ref, cos_ref, sin_ref, dmask_ref, cross_ref, tail_ref,
                      cdec_ref, o_ref, sfin_ref, s_scr, *, n_sub, chunk, k_scale):
    c = pl.program_id(2)

    @pl.when(c == 0)
    def _():
        s_scr[...] = jnp.zeros_like(s_scr)

    dmask = dmask_ref[0]
    cross = cross_ref[0]
    tail = tail_ref[0]
    cdec = cdec_ref[0]
    for j in range(n_sub):
        rows = pl.ds(j * chunk, chunk)
        cos = cos_ref[rows, :]
        sin = sin_ref[rows, :]
        qc = _rotary(q_ref[rows, :], cos, sin)
        kc = _rotary(k_ref[rows, :], cos, sin) * k_scale
        vb = v_ref[rows, :].astype(BF16)
        qb = qc.astype(BF16)
        att = _dot_nt(qb, kc.astype(BF16)) * dmask
        s = s_scr[...]
        o = _dot(att.astype(BF16), vb) + _dot(qb, s.astype(BF16)) * cross
        s_scr[...] = s * cdec + _dot_tn((kc * tail).astype(BF16), vb)
        o_ref[rows, :] = _group_norm_gate(o, g_ref[rows, :]).astype(BF16)

    @pl.when(c == pl.num_programs(2) - 1)
    def _():
        sfin_ref[0, 0] = s_scr[...]


def _ret_tables(heads, chunk):
    lg = jnp.log1p(-jnp.exp2(-5.0 - jnp.arange(heads, dtype=F32)))
    i = jnp.arange(chunk, dtype=F32)
    diff = i[:, None] - i[None, :]
    dmask = jnp.where(diff >= 0, jnp.exp(lg[:, None, None] * jnp.maximum(diff, 0.0)), 0.0)
    cross = jnp.exp(lg[:, None] * (i[None, :] + 1.0))[:, :, None]
    tail = jnp.exp(lg[:, None] * (chunk - 1.0 - i[None, :]))[:, :, None]
    cdec = jnp.exp(lg * chunk)
    return dmask, cross, tail, cdec


def _rope_tables(pos, half):
    inv = ROPE_BASE ** (-jnp.arange(half, dtype=F32) / half)
    ang = pos.astype(F32)[:, None] * inv[None, :]
    return jnp.cos(ang), jnp.sin(ang)


def _retention_prompt(proj, batch, seq, d):
    heads = RET_HEADS
    dk = d // heads
    dv = 2 * dk
    chunk = RET_CHUNK
    rows = _row_tile(seq, 4 * chunk)
    n_sub = rows // chunk
    nblk = seq // rows
    cos, sin = _rope_tables(jnp.arange(seq, dtype=jnp.int32), dk // 2)
    dmask, cross, tail, cdec = _ret_tables(heads, chunk)
    cdec = jnp.broadcast_to(cdec[:, None, None], (heads, 1, dv))
    kq = heads
    vq = (2 * heads * dk) // dv
    gq = vq + heads
    o, sfin = pl.pallas_call(
        functools.partial(_ret_chunk_kernel, n_sub=n_sub, chunk=chunk, k_scale=dk ** -0.5),
        out_shape=[jax.ShapeDtypeStruct((batch * seq, heads * dv), BF16),
                   jax.ShapeDtypeStruct((batch, heads, dk, dv), F32)],
        grid=(batch, heads, nblk),
        in_specs=[
            pl.BlockSpec((rows, dk), lambda b, h, c: (b * nblk + c, h)),
            pl.BlockSpec((rows, dk), lambda b, h, c: (b * nblk + c, kq + h)),
            pl.BlockSpec((rows, dv), lambda b, h, c: (b * nblk + c, vq + h)),
            pl.BlockSpec((rows, dv), lambda b, h, c: (b * nblk + c, gq + h)),
            pl.BlockSpec((rows, dk // 2), lambda b, h, c: (c, 0)),
            pl.BlockSpec((rows, dk // 2), lambda b, h, c: (c, 0)),
            pl.BlockSpec((1, chunk, chunk), lambda b, h, c: (h, 0, 0)),
            pl.BlockSpec((1, chunk, 1), lambda b, h, c: (h, 0, 0)),
            pl.BlockSpec((1, chunk, 1), lambda b, h, c: (h, 0, 0)),
            pl.BlockSpec((1, 1, dv), lambda b, h, c: (h, 0, 0)),
        ],
        out_specs=[
            pl.BlockSpec((rows, dv), lambda b, h, c: (b * nblk + c, h)),
            pl.BlockSpec((1, 1, dk, dv), lambda b, h, c: (b, h, 0, 0)),
        ],
        scratch_shapes=[pltpu.VMEM((dk, dv), F32)],
        compiler_params=_cparams("parallel", "parallel", "arbitrary"),
        name="retention_chunks",
    )(proj, proj, proj, proj, cos, sin, dmask, cross, tail, cdec)
    return o, sfin


def _ret_step_kernel(q_ref, k_ref, v_ref, g_ref, cos_ref, sin_ref, gam_ref, s0_ref, *rest, nb, k_scale):
    o_ref, s_ref = rest[-2:]
    cos = cos_ref[...]
    sin = sin_ref[...]
    gam = gam_ref[0]
    qc = _rotary(q_ref[...], cos, sin)
    kc = _rotary(k_ref[...], cos, sin) * k_scale
    qr = qc.astype(BF16).astype(F32)
    kr = kc.astype(BF16).astype(F32)
    vr = v_ref[...].astype(BF16).astype(F32)
    att = jnp.sum(qr * kr, axis=-1, keepdims=True)
    att = att.astype(BF16).astype(F32)
    dk = qc.shape[1]
    eye = (lax.broadcasted_iota(jnp.int32, (dk, dk), 0) == lax.broadcasted_iota(jnp.int32, (dk, dk), 1))
    rows8 = lax.broadcasted_iota(jnp.int32, (V7X_SUBLANES, dk), 0)
    o = jnp.zeros(vr.shape, F32)
    brow = lax.broadcasted_iota(jnp.int32, vr.shape, 0)
    for b in range(nb):
        s0 = s0_ref[0, b, 0]
        qrow = qc[b:b + 1, :]
        q8 = jnp.where(rows8 == 0, qrow, 0.0).astype(BF16)
        qs = _dot(q8, s0.astype(BF16))[0:1, :]
        o = jnp.where(brow == b, att[b:b + 1, :] * vr[b:b + 1, :] + qs * gam, o)
        kcol = jnp.sum(jnp.where(eye, kr[b:b + 1, :], 0.0), axis=1, keepdims=True)
        s_ref[0, b, 0] = s0 * gam + kcol * vr[b:b + 1, :]
    o_ref[...] = _group_norm_gate(o, g_ref[...]).astype(BF16)


def _retention_step(proj, states, layer, new_states, d):
    t = proj.shape[0]
    heads = RET_HEADS
    dk = d // heads
    dv = 2 * dk
    nb = V7X_SUBLANES
    cos, sin = _rope_tables(PAST_LEN + jnp.arange(1, dtype=jnp.int32), dk // 2)
    lg = jnp.log1p(-jnp.exp2(-5.0 - jnp.arange(heads, dtype=F32)))
    gam = jnp.broadcast_to(jnp.exp(lg)[:, None, None], (heads, 1, dv))
    kq = heads
    vq = (2 * heads * dk) // dv
    gq = vq + heads
    state_spec = pl.BlockSpec((1, nb, 1, dk, dv), lambda i, h: (layer, i, h, 0, 0))
    in_specs = [
        pl.BlockSpec((nb, dk), lambda i, h: (i, h)),
        pl.BlockSpec((nb, dk), lambda i, h: (i, kq + h)),
        pl.BlockSpec((nb, dv), lambda i, h: (i, vq + h)),
        pl.BlockSpec((nb, dv), lambda i, h: (i, gq + h)),
        pl.BlockSpec((1, dk // 2), lambda i, h: (0, 0)),
        pl.BlockSpec((1, dk // 2), lambda i, h: (0, 0)),
        pl.BlockSpec((1, 1, dv), lambda i, h: (h, 0, 0)),
        state_spec,
    ]
    args = [proj, proj, proj, proj, cos, sin, gam, states]
    aliases = {}
    if new_states is not None:
        in_specs.append(pl.BlockSpec(memory_space=pl.ANY))
        args.append(new_states)
        aliases = {len(args) - 1: 1}
    o, s = pl.pallas_call(
        functools.partial(_ret_step_kernel, nb=nb, k_scale=dk ** -0.5),
        out_shape=[jax.ShapeDtypeStruct((t, heads * dv), BF16),
                   jax.ShapeDtypeStruct(states.shape, F32)],
        grid=(t // nb, heads),
        in_specs=in_specs,
        out_specs=[pl.BlockSpec((nb, dv), lambda i, h: (i, h)), state_spec],
        input_output_aliases=aliases,
        compiler_params=_cparams("parallel", "parallel"),
        name="retention_step",
    )(*args)
    return o, s


def _ssm_disc_kernel(lr_ref, li_ref, ldt_ref, br_ref, bi_ref, are_ref, aim_ref, bbr_ref, bbi_ref):
    lr = lr_ref[...]
    li = li_ref[...]
    dt = jnp.exp(ldt_ref[...])
    mag = jnp.exp(lr * dt)
    a_re = mag * jnp.cos(li * dt)
    a_im = mag * jnp.sin(li * dt)
    den = lr * lr + li * li
    nr = a_re - 1.0
    c_re = (nr * lr + a_im * li) / den
    c_im = (a_im * lr - nr * li) / den
    are_ref[...] = a_re
    aim_ref[...] = a_im
    br = br_ref[...]
    bi = bi_ref[...]
    bbr_ref[...] = c_re * br - c_im * bi
    bbi_ref[...] = c_re * bi + c_im * br


def _ssm_discretize(lam_re, lam_im, log_dt, b_re, b_im):
    g, n = lam_re.shape
    p = b_re.shape[2]
    brt = jnp.transpose(b_re, (0, 2, 1))
    bit = jnp.transpose(b_im, (0, 2, 1))
    a_re, a_im, bbr, bbi = pl.pallas_call(
        _ssm_disc_kernel,
        out_shape=[jax.ShapeDtypeStruct((g, 1, n), F32), jax.ShapeDtypeStruct((g, 1, n), F32),
                   jax.ShapeDtypeStruct((g, p, n), F32), jax.ShapeDtypeStruct((g, p, n), F32)],
        name="ssm_discretize",
    )(lam_re.reshape(g, 1, n), lam_im.reshape(g, 1, n), log_dt.reshape(g, 1, 1), brt, bit)
    return a_re.reshape(g, n), a_im.reshape(g, n), bbr, bbi


def _block_diag(w, per):
    g, a, b = w.shape
    w = w.reshape(g // per, per, a, b)
    eye = jnp.eye(per, dtype=w.dtype)
    return jnp.einsum('cgab,gh->cgahb', w, eye).reshape(g // per, per * a, per * b)


def _ssm_bu(ub, bdr_ref, bdi_ref, nblk, kw):
    bur = jnp.concatenate([_dot(ub[:, c * kw:(c + 1) * kw], bdr_ref[c]) for c in range(nblk)], axis=1)
    bui = jnp.concatenate([_dot(ub[:, c * kw:(c + 1) * kw], bdi_ref[c]) for c in range(nblk)], axis=1)
    return bur, bui


def _ssm_y(hr, hi, cdr_ref, cdi_ref, nblk, kw):
    hrb = hr.astype(BF16)
    hib = hi.astype(BF16)
    return jnp.concatenate(
        [_dot(hrb[:, c * kw:(c + 1) * kw], cdr_ref[c]) - _dot(hib[:, c * kw:(c + 1) * kw], cdi_ref[c])
         for c in range(nblk)], axis=1)


def _ssm_scan_kernel(u_ref, bdr_ref, bdi_ref, cdr_ref, cdi_ref, d_ref, ar_ref, ai_ref,
                     z_ref, fr_ref, fi_ref, hr_scr, hi_scr, cr_scr, ci_scr, *, nblk, nb, lane_chunk):
    step = pl.program_id(0)

    @pl.when(step == 0)
    def _():
        cr_scr[...] = jnp.zeros_like(cr_scr)
        ci_scr[...] = jnp.zeros_like(ci_scr)

    u = u_ref[...]
    kw_in = u.shape[1] // nblk
    bur, bui = _ssm_bu(u.astype(BF16), bdr_ref, bdi_ref, nblk, kw_in)
    hr_scr[...] = bur
    hi_scr[...] = bui

    rows_total, gn = hr_scr.shape
    for c in range(gn // lane_chunk):
        lanes = pl.ds(c * lane_chunk, lane_chunk)
        ar = jnp.broadcast_to(ar_ref[:, lanes], (nb, lane_chunk))
        ai = jnp.broadcast_to(ai_ref[:, lanes], (nb, lane_chunk))

        def time_step(s, carry):
            hr, hi = carry
            rows = pl.ds(pl.multiple_of(s * nb, nb), nb)
            nr = hr_scr[rows, lanes] + (ar * hr - ai * hi)
            ni = hi_scr[rows, lanes] + (ar * hi + ai * hr)
            hr_scr[rows, lanes] = nr
            hi_scr[rows, lanes] = ni
            return nr, ni

        hr, hi = lax.fori_loop(0, rows_total // nb, time_step, (cr_scr[:, lanes], ci_scr[:, lanes]))
        cr_scr[:, lanes] = hr
        ci_scr[:, lanes] = hi

    kw_out = gn // nblk
    y = _ssm_y(hr_scr[...], hi_scr[...], cdr_ref, cdi_ref, nblk, kw_out) + d_ref[...] * u
    z_ref[...] = _gelu_tanh(y).astype(BF16)

    @pl.when(step == pl.num_programs(0) - 1)
    def _():
        fr_ref[...] = cr_scr[...]
        fi_ref[...] = ci_scr[...]


def _ssm_prompt(u_tm, disc, batch, seq):
    t, d = u_tm.shape
    bdr, bdi, cdr, cdi, dvec, a_re, a_im = disc
    nblk = bdr.shape[0]
    gn = a_re.shape[1]
    assert batch == V7X_SUBLANES, "one time step of all sequences must fill the sublanes of a vreg"
    rows = batch * _row_tile(seq, 32)
    full = lambda a: pl.BlockSpec(a.shape, lambda s: (0,) * a.ndim)
    z, fr, fi = pl.pallas_call(
        functools.partial(_ssm_scan_kernel, nblk=nblk, nb=batch, lane_chunk=4 * V7X_LANES),
        out_shape=[jax.ShapeDtypeStruct((t, d), BF16),
                   jax.ShapeDtypeStruct((batch, gn), F32),
                   jax.ShapeDtypeStruct((batch, gn), F32)],
        grid=(t // rows,),
        in_specs=[pl.BlockSpec((rows, d), lambda s: (s, 0)),
                  full(bdr), full(bdi), full(cdr), full(cdi), full(dvec), full(a_re), full(a_im)],
        out_specs=[pl.BlockSpec((rows, d), lambda s: (s, 0)),
                   pl.BlockSpec((batch, gn), lambda s: (0, 0)),
                   pl.BlockSpec((batch, gn), lambda s: (0, 0))],
        scratch_shapes=[pltpu.VMEM((rows, gn), F32), pltpu.VMEM((rows, gn), F32),
                        pltpu.VMEM((batch, gn), F32), pltpu.VMEM((batch, gn), F32)],
        compiler_params=_cparams("arbitrary"),
        name="ssm_scan",
    )(u_tm, bdr, bdi, cdr, cdi, dvec, a_re, a_im)
    return z, fr, fi


def _ssm_step_kernel(u_ref, h0r_ref, h0i_ref, bdr_ref, bdi_ref, cdr_ref, cdi_ref, d_ref, ar_ref, ai_ref,
                     z_ref, fr_ref, fi_ref, *, nblk):
    u = u_ref[...]
    kw_in = u.shape[1] // nblk
    bur, bui = _ssm_bu(u.astype(BF16), bdr_ref, bdi_ref, nblk, kw_in)
    ar = ar_ref[...]
    ai = ai_ref[...]
    h0r = h0r_ref[...]
    h0i = h0i_ref[...]
    hr = bur + (ar * h0r - ai * h0i)
    hi = bui + (ar * h0i + ai * h0r)
    fr_ref[...] = hr
    fi_ref[...] = hi
    kw_out = hr.shape[1] // nblk
    y = _ssm_y(hr, hi, cdr_ref, cdi_ref, nblk, kw_out) + d_ref[...] * u
    z_ref[...] = _gelu_tanh(y).astype(BF16)


def _ssm_step(u, h0r, h0i, disc):
    t, d = u.shape
    bdr, bdi, cdr, cdi, dvec, a_re, a_im = disc
    gn = a_re.shape[1]
    return pl.pallas_call(
        functools.partial(_ssm_step_kernel, nblk=bdr.shape[0]),
        out_shape=[jax.ShapeDtypeStruct((t, d), BF16),
                   jax.ShapeDtypeStruct((t, gn), F32), jax.ShapeDtypeStruct((t, gn), F32)],
        name="ssm_step",
        compiler_params=pltpu.CompilerParams(vmem_limit_bytes=V7X_VMEM_LIMIT_BYTES),
    )(u, h0r, h0i, bdr, bdi, cdr, cdi, dvec, a_re, a_im)


def _peer_scores_kernel(q_ref, key_ref, s_ref, *, nkh, nk, dq):
    q = q_ref[...].astype(BF16)
    for c in range(nkh):
        s_ref[c] = _dot_nt(key_ref[c], q[:, c * dq:(c + 1) * dq])


def _peer_scores(q, keys):
    t, n = q.shape
    nkh, nk, dq = keys.shape
    tm = _row_tile(t, 512)
    return pl.pallas_call(
        functools.partial(_peer_scores_kernel, nkh=nkh, nk=nk, dq=dq),
        out_shape=jax.ShapeDtypeStruct((nkh, nk, t), F32),
        grid=(t // tm,),
        in_specs=[pl.BlockSpec((tm, n), lambda i: (i, 0)),
                  pl.BlockSpec((nkh, nk, dq), lambda i: (0, 0, 0))],
        out_specs=pl.BlockSpec((nkh, nk, tm), lambda i: (0, 0, i)),
        compiler_params=_cparams("parallel"),
        name="peer_scores",
    )(q, keys)


def _remove_max_rounds(s, k, idx, want_round):
    cur = s
    rnd = jnp.full(s.shape, float(k), F32) if want_round else None
    vals = []
    for a in range(k):
        m = jnp.max(cur, axis=0, keepdims=True)
        if idx is None:
            hit = cur == m
        else:
            first = jnp.min(jnp.where(cur == m, idx, float(s.shape[0])), axis=0, keepdims=True)
            hit = idx == first
        if want_round:
            rnd = jnp.where(hit, float(a), rnd)
        cur = jnp.where(hit, NEG_INF, cur)
        vals.append(m)
    return vals, rnd, cur


def _rows_from_list(vals, lo, n, ridx):
    out = jnp.zeros(ridx.shape, F32)
    for r in range(n):
        out = jnp.where(ridx == float(r), vals[lo + r], out)
    return out


def _peer_select_one(s1, s2, topk, idx, cidx, valid, ridx_k, ridx_s):
    sub = V7X_SUBLANES
    stair = [topk // (a + 1) for a in range(topk)]
    v1, rank1, cur1 = _remove_max_rounds(s1, topk, idx, True)
    v2, rank2, cur2 = _remove_max_rounds(s2, topk, idx, True)
    v2all = _rows_from_list(v2, 0, topk, ridx_k)
    v2lo = _rows_from_list(v2, 0, sub, ridx_s)
    v1hi = _rows_from_list(v1, sub, topk - sub, ridx_s)
    blocks = [v1[0] + v2all] + [v1[a] + v2lo for a in range(1, sub)] + [v1hi + v2[0]]
    cand = jnp.where(valid, jnp.concatenate(blocks, axis=0), NEG_INF)
    tops, _, curc = _remove_max_rounds(cand, topk, cidx, False)
    taken = jnp.where(jnp.logical_and(curc == NEG_INF, valid), 1.0, 0.0)
    z = jnp.ones_like(tops[0])
    for m in tops[1:]:
        z = z + jnp.exp(m - tops[0])
    n1 = jnp.zeros(s1.shape, F32)
    total = jnp.zeros_like(z)
    for a in range(topk):
        if a == 0:
            n_a = jnp.sum(taken[0:topk, :], axis=0, keepdims=True)
        elif a < sub:
            base = topk + sub * (a - 1)
            n_a = jnp.sum(taken[base:base + sub, :], axis=0, keepdims=True)
        else:
            row = topk + sub * (sub - 1) + (a - sub)
            n_a = taken[row:row + 1, :]
        total = total + n_a
        n1 = jnp.where(rank1 == float(a), n_a, n1)
    if idx is None:
        k = float(topk)
        c1 = jnp.sum(jnp.where(cur1 == NEG_INF, 1.0, 0.0), axis=0, keepdims=True)
        c2 = jnp.sum(jnp.where(cur2 == NEG_INF, 1.0, 0.0), axis=0, keepdims=True)
        bad = jnp.where(jnp.logical_and(jnp.logical_and(c1 == k, c2 == k), total == k), 0.0, 1.0)
    else:
        bad = None
    e2 = jnp.exp(s2 - v2[0])
    e1 = jnp.exp(s1 - v1[0]) / z
    return rank2, e2, n1, e1, bad


def _peer_select_kernel(s_ref, r2_ref, e2_ref, n1_ref, e1_ref, *, heads, topk):
    sub = V7X_SUBLANES
    nk = s_ref.shape[1]
    lanes = s_ref.shape[2]
    ncand = topk + sub * sub
    idx = lax.broadcasted_iota(jnp.int32, (nk, lanes), 0).astype(F32)
    crow = lax.broadcasted_iota(jnp.int32, (ncand, lanes), 0)
    cidx = crow.astype(F32)
    ridx_k = lax.broadcasted_iota(jnp.int32, (topk, lanes), 0).astype(F32)
    ridx_s = lax.broadcasted_iota(jnp.int32, (sub, lanes), 0).astype(F32)
    valid = crow < topk + sub
    for a in range(2, sub):
        base = topk + sub * (a - 1)
        valid = jnp.logical_or(valid, jnp.logical_and(crow >= base, crow < base + topk // (a + 1)))
    valid = jnp.logical_or(valid, crow >= topk + sub * (sub - 1))

    def write(h, res):
        rank2, e2, n1, e1 = res
        r2_ref[h] = rank2.astype(r2_ref.dtype)
        e2_ref[h] = e2.astype(e2_ref.dtype)
        n1_ref[h] = n1
        e1_ref[h] = e1

    def one_head(h):
        s1 = s_ref[2 * h]
        s2 = s_ref[2 * h + 1]
        *res, bad = _peer_select_one(s1, s2, topk, None, None, valid, ridx_k, ridx_s)
        write(h, res)

        @pl.when(jnp.max(bad) > 0.0)
        def _():
            *res, _ = _peer_select_one(s1, s2, topk, idx, cidx, valid, ridx_k, ridx_s)
            write(h, res)

    def head_pair(hp, carry):
        one_head(2 * hp)
        one_head(2 * hp + 1)
        return carry

    lax.fori_loop(0, heads // 2, head_pair, 0)


def _peer_select(scores, heads, topk):
    nkh, nk, t = scores.shape
    tl = _row_tile(t, V7X_LANES)
    shp = jax.ShapeDtypeStruct((heads, nk, t), F32)
    shp_i2 = jax.ShapeDtypeStruct((heads, nk, t), BF16)
    ospec = pl.BlockSpec((heads, nk, tl), lambda i: (0, 0, i))
    return pl.pallas_call(
        functools.partial(_peer_select_kernel, heads=heads, topk=topk),
        out_shape=[shp_i2, shp_i2, shp, shp],
        grid=(t // tl,),
        in_specs=[pl.BlockSpec((nkh, nk, tl), lambda i: (0, 0, i))],
        out_specs=[ospec, ospec, ospec, ospec],
        compiler_params=_cparams("parallel"),
        name="peer_select",
    )(scores)


def _peer_dense_kernel(ht_ref, u_ref, vt_ref, r2_ref, e2_ref, n1_ref, e1_ref, x_ref, gate_ref, o_ref,
                       at0_scr, at1_scr, wt_scr, acc_scr, *, heads, nk, i1_per_step, nblk):
    j = pl.program_id(1)
    at_scr = (at0_scr, at1_scr)

    def activations(dst):
        dst[...] = _dot(u_ref[...], ht_ref[...])

    pack = 2 * V7X_SUBLANES
    tn = wt_scr.shape[1]

    def packed_row(ref, h, r):
        return jnp.broadcast_to(ref[h, r:r + 1, :], (pack, tn)).astype(BF16)[None]

    def mix(src):
        for r in range(i1_per_step):
            rows = pl.ds(r * nk, nk)
            g = None
            for h in range(heads):
                n = packed_row(n1_ref, h, r)
                e1 = packed_row(e1_ref, h, r)
                r2 = r2_ref[h].reshape(nk // pack, pack, tn)
                e2 = e2_ref[h].reshape(nk // pack, pack, tn)
                term = jnp.where(r2 < n, e2, jnp.zeros_like(e2)) * e1
                g = term if g is None else g + term
            g = g.reshape(nk, tn).astype(F32)
            wt_scr[rows, :] = (_gelu_tanh(src[rows, :]) * g).astype(BF16)
        acc_scr[...] += _dot(vt_ref[...], wt_scr[...])

    @pl.when(j == 0)
    def _():
        acc_scr[...] = jnp.zeros_like(acc_scr)
        activations(at_scr[0])

    for parity in range(2):
        @pl.when(jnp.logical_and(jnp.logical_and(j > 0, j < nblk), j % 2 == parity))
        def _():
            activations(at_scr[parity])
            mix(at_scr[1 - parity])

    @pl.when(j == nblk)
    def _():
        mix(at_scr[(nblk - 1) % 2])
        o_ref[...] = x_ref[...] + gate_ref[0] * jnp.transpose(acc_scr[...])


def _peer_dense(ht, u, vt, r2, e2, n1, e1, x, gate, seq):
    t, d = x.shape
    ne = u.shape[0]
    heads, nk, _ = r2.shape
    i1_per_step = V7X_SUBLANES
    te = i1_per_step * nk
    nblk = ne // te
    tn = _row_tile(min(t, seq) if seq > 1 else t, 512)
    clamp = lambda b: jnp.clip(b, 0, nblk - 1)
    tok = pl.BlockSpec((heads, nk, tn), lambda i, j: (0, 0, i))
    per_i1 = pl.BlockSpec((heads, i1_per_step, tn), lambda i, j: (0, clamp(j - 1), i))
    return pl.pallas_call(
        functools.partial(_peer_dense_kernel, heads=heads, nk=nk, i1_per_step=i1_per_step, nblk=nblk),
        out_shape=jax.ShapeDtypeStruct((t, d), F32),
        grid=(t // tn, nblk + 1),
        in_specs=[
            pl.BlockSpec((d, tn), lambda i, j: (0, i)),
            pl.BlockSpec((te, d), lambda i, j: (clamp(j), 0)),
            pl.BlockSpec((d, te), lambda i, j: (0, clamp(j - 1))),
            tok, tok, per_i1, per_i1,
            pl.BlockSpec((tn, d), lambda i, j: (i, 0)),
            _mod_specs(tn, d, seq),
        ],
        out_specs=pl.BlockSpec((tn, d), lambda i, j: (i, 0)),
        scratch_shapes=[pltpu.VMEM((te, tn), F32), pltpu.VMEM((te, tn), F32),
                        pltpu.VMEM((te, tn), BF16), pltpu.VMEM((d, tn), F32)],
        compiler_params=_cparams("parallel", "arbitrary"),
        name="peer_dense",
    )(ht, u, vt, r2, e2, n1, e1, x, gate)


def _rmsnorm_kernel(x_ref, g_ref, o_ref):
    x = x_ref[...]
    o_ref[...] = (x * lax.rsqrt(jnp.mean(x * x, axis=-1, keepdims=True) + EPS)) * g_ref[...]


def _rmsnorm(x, g):
    t, d = x.shape
    tm = _row_tile(t, 1024)
    return pl.pallas_call(
        _rmsnorm_kernel,
        out_shape=jax.ShapeDtypeStruct((t, d), F32),
        grid=(t // tm,),
        in_specs=[pl.BlockSpec((tm, d), lambda i: (i, 0)), pl.BlockSpec((1, d), lambda i: (0, 0))],
        out_specs=pl.BlockSpec((tm, d), lambda i: (i, 0)),
        compiler_params=_cparams("parallel"),
        name="final_rmsnorm",
    )(x, g)


def _mods(mod, lo, hi, seq):
    d = mod.shape[1] // 6
    rows = mod[lo:hi]
    parts = [rows[:, k * d:(k + 1) * d] for k in range(6)]
    if seq == 1:
        return [p[None, :, :] for p in parts]
    return [p[:, None, :] for p in parts]


def kernel(x_prompt, x_sample, c_prompt, c_sample, state_ret, state_ssm_re, state_ssm_im, norm_g, final_g, w_ada, b_ada, ret_w_in, ret_w_out, ssm_w_in, ssm_lam_re, ssm_lam_im, ssm_log_dt, ssm_b_re, ssm_b_im, ssm_c_re, ssm_c_im, ssm_d, ssm_w_glu, peer_w_q, peer_key1, peer_key2, peer_u, peer_v):
    batch, seq, d = x_prompt.shape
    dbatch, dseq, _ = x_sample.shape
    assert dseq == 1 and seq % RET_CHUNK == 0
    depth = w_ada.shape[0]
    heads = peer_key1.shape[1]

    c_all = jnp.concatenate([c_prompt, c_sample], axis=0)
    pad = (-c_all.shape[0]) % V7X_SUBLANES
    c_all = jnp.pad(c_all, ((0, pad), (0, 0)))
    mod = _ada(c_all, w_ada.astype(BF16), b_ada)

    groups = [
        dict(x=x_prompt.reshape(batch * seq, d), lo=0, hi=batch, seq=seq, nb=batch),
        dict(x=x_sample.reshape(dbatch, d), lo=batch, hi=batch + dbatch, seq=1, nb=dbatch),
    ]
    outs = [dict(ret=[], re=[], im=[]) for _ in groups]
    ret_sample = None

    for i in range(depth):
        jm = i // N_MIXERS
        is_ret = i % N_MIXERS == 0
        w_q = peer_w_q[i].astype(BF16)
        keys = jnp.stack([peer_key1[i], peer_key2[i]], axis=1).reshape(2 * heads, PEER_NKEYS, -1).astype(BF16)
        u_tab = peer_u[i].astype(BF16)
        vt_tab = jnp.transpose(peer_v[i]).astype(BF16)
        g_mix = norm_g[i, 0].reshape(1, d)
        g_peer = norm_g[i, 1].reshape(1, d)
        if is_ret:
            w_in = ret_w_in[jm].astype(BF16)
            w_out = ret_w_out[jm].astype(BF16)
        else:
            w_in = ssm_w_in[jm].astype(BF16)
            w_out = ssm_w_glu[jm].astype(BF16)
            a_re, a_im, bbr, bbi = _ssm_discretize(ssm_lam_re[jm], ssm_lam_im[jm], ssm_log_dt[jm],
                                                   ssm_b_re[jm], ssm_b_im[jm])
            per = V7X_MXU_DIM // SSM_GROUP
            disc = (_block_diag(bbr, per).astype(BF16), _block_diag(bbi, per).astype(BF16),
                    _block_diag(jnp.transpose(ssm_c_re[jm], (0, 2, 1)), per).astype(BF16),
                    _block_diag(jnp.transpose(ssm_c_im[jm], (0, 2, 1)), per).astype(BF16),
                    ssm_d[jm].reshape(1, d), a_re.reshape(1, -1), a_im.reshape(1, -1))

        for gi, grp in enumerate(groups):
            x = grp["x"]
            sq = grp["seq"]
            sh1, sc1, g1, sh2, sc2, g2 = _mods(mod[i], grp["lo"], grp["hi"], sq)
            tmaj = sq > 1 and not is_ret
            proj = _nm_matmul(x, g_mix, sh1, sc1, w_in, sq, time_major=tmaj)
            if is_ret:
                if sq > 1:
                    y, s = _retention_prompt(proj, grp["nb"], sq, d)
                    outs[gi]["ret"].append(s)
                else:
                    y, ret_sample = _retention_step(proj, state_ret, jm, ret_sample, d)
                x = _matmul_res(y, w_out, x, g1, sq, glu=False)
            else:
                if sq > 1:
                    z, fr, fi = _ssm_prompt(proj, disc, grp["nb"], sq)
                else:
                    z, fr, fi = _ssm_step(proj, state_ssm_re[jm].reshape(dbatch, -1),
                                          state_ssm_im[jm].reshape(dbatch, -1), disc)
                outs[gi]["re"].append(fr.reshape(grp["nb"], -1, SSM_STATE))
                outs[gi]["im"].append(fi.reshape(grp["nb"], -1, SSM_STATE))
                x = _matmul_res(z, w_out, x, g1, sq, glu=True, a_time_major=tmaj)
            q, h2 = _nm_matmul(x, g_peer, sh2, sc2, w_q, sq, emit_h=True)
            scores = _peer_scores(q, keys)
            r2, e2, n1, e1 = _peer_select(scores, heads, PEER_TOPK)
            x = _peer_dense(h2, u_tab, vt_tab, r2, e2, n1, e1, x, g2, sq)
            grp["x"] = x

    fg = final_g.reshape(1, d)
    y_prompt = _rmsnorm(groups[0]["x"], fg).reshape(batch, seq, d)
    y_sample = _rmsnorm(groups[1]["x"], fg).reshape(dbatch, 1, d)
    return (y_prompt, y_sample,
            jnp.stack(outs[0]["ret"]), ret_sample,
            jnp.stack(outs[0]["re"]), jnp.stack(outs[0]["im"]),
            jnp.stack(outs[1]["re"]), jnp.stack(outs[1]["im"]))
```

```python
import functools
import math

import jax
import jax.numpy as jnp
import numpy as np
from jax import lax
from jax.experimental import pallas as pl
from jax.experimental.pallas import tpu as pltpu

F32 = jnp.float32
BF16 = jnp.bfloat16

EPS = 1e-6
ROPE_BASE = 10000.0
PAST_LEN = 16384
RET_HEADS = 4
RET_CHUNK = 128
SSM_GROUP = 16
SSM_STATE = 64
PEER_HEADS = 8
PEER_NKEYS = 128
PEER_TOPK = 16
N_MIXERS = 2

V7X_LANES = 128
V7X_SUBLANES = 8
V7X_MXU_DIM = 256
V7X_VMEM_LIMIT_BYTES = 56 * 1024 * 1024

NEG_INF = float("-inf")


def _cparams(*sem):
    return pltpu.CompilerParams(dimension_semantics=sem, vmem_limit_bytes=V7X_VMEM_LIMIT_BYTES)


def _dot(a, b):
    return jnp.dot(a, b, preferred_element_type=F32)


def _dot_nt(a, b):
    return lax.dot_general(a, b, (((1,), (1,)), ((), ())), preferred_element_type=F32)


def _dot_tn(a, b):
    return lax.dot_general(a, b, (((0,), (0,)), ((), ())), preferred_element_type=F32)


def _sigmoid(x):
    return 1.0 / (1.0 + jnp.exp(-x))


def _gelu_tanh(x):
    c = math.sqrt(2.0 / math.pi)
    hx = 0.5 * x
    return hx + hx * jnp.tanh(x * (c + (c * 0.044715) * (x * x)))


def _row_tile(t, pref):
    if t <= pref:
        return t
    tile = pref
    while t % tile:
        tile //= 2
    return tile


def _ada_kernel(c_ref, w_ref, b_ref, o_ref):
    c = c_ref[...]
    sc = (c * _sigmoid(c)).astype(BF16)
    o_ref[0] = _dot(sc, w_ref[0]) + b_ref[0]


def _ada(c_all, w_ada, b_ada):
    m, d = c_all.shape
    depth, _, n = w_ada.shape
    tn = 1024
    return pl.pallas_call(
        _ada_kernel,
        out_shape=jax.ShapeDtypeStruct((depth, m, n), F32),
        grid=(depth, n // tn),
        in_specs=[
            pl.BlockSpec((m, d), lambda l, j: (0, 0)),
            pl.BlockSpec((1, d, tn), lambda l, j: (l, 0, j)),
            pl.BlockSpec((1, 1, tn), lambda l, j: (l, 0, j)),
        ],
        out_specs=pl.BlockSpec((1, m, tn), lambda l, j: (l, 0, j)),
        compiler_params=_cparams("parallel", "parallel"),
        name="ada_mod",
    )(c_all, w_ada, b_ada.reshape(depth, 1, n))


def _norm_mod(x, g, sh, sc):
    y = x * lax.rsqrt(jnp.mean(x * x, axis=-1, keepdims=True) + EPS)
    return (y * g) * (1.0 + sc) + sh


def _nm_matmul_kernel(x_ref, g_ref, sh_ref, sc_ref, w_ref, o_ref, *rest, emit_h):
    if emit_h:
        h_ref, h_scr = rest
    else:
        (h_scr,) = rest
    j = pl.program_id(1)

    @pl.when(j == 0)
    def _():
        h = _norm_mod(x_ref[...], g_ref[...], sh_ref[0], sc_ref[0])
        h_scr[...] = h.astype(BF16)
        if emit_h:
            h_ref[...] = jnp.transpose(h).astype(BF16)

    o_ref[...] = _dot(h_scr[...], w_ref[...])


def _mod_specs(tm, d, seq):
    if seq == 1:
        return pl.BlockSpec((1, tm, d), lambda i, *_: (0, i, 0))
    per = seq // tm
    return pl.BlockSpec((1, 1, d), lambda i, *_: (i // per, 0, 0))


def _nm_matmul(x, g, sh, sc, w, seq, emit_h=False, time_major=False, tm_pref=512, tn_pref=1024):
    t, d = x.shape
    n = w.shape[1]
    tm = _row_tile(min(t, seq) if seq > 1 else t, tm_pref)
    tn = min(n, tn_pref)
    if time_major:
        per, nj = seq // tm, n // tn
        out_shape = [jax.ShapeDtypeStruct((seq, (t // seq) * n), F32)]
        out_specs = [pl.BlockSpec((tm, tn), lambda i, j: (i % per, (i // per) * nj + j))]
    else:
        out_shape = [jax.ShapeDtypeStruct((t, n), F32)]
        out_specs = [pl.BlockSpec((tm, tn), lambda i, j: (i, j))]
    if emit_h:
        out_shape.append(jax.ShapeDtypeStruct((d, t), BF16))
        out_specs.append(pl.BlockSpec((d, tm), lambda i, j: (0, i)))
    res = pl.pallas_call(
        functools.partial(_nm_matmul_kernel, emit_h=emit_h),
        out_shape=out_shape,
        grid=(t // tm, n // tn),
        in_specs=[
            pl.BlockSpec((tm, d), lambda i, j: (i, 0)),
            pl.BlockSpec((1, d), lambda i, j: (0, 0)),
            _mod_specs(tm, d, seq),
            _mod_specs(tm, d, seq),
            pl.BlockSpec((d, tn), lambda i, j: (0, j)),
        ],
        out_specs=out_specs,
        scratch_shapes=[pltpu.VMEM((tm, d), BF16)],
        compiler_params=_cparams("parallel", "arbitrary"),
        name="norm_mod_matmul",
    )(x, g, sh, sc, w)
    res = list(res)
    if time_major:
        res[0] = res[0].reshape(t, n)
    return res if emit_h else res[0]


def _matmul_res_kernel(a_ref, w_ref, x_ref, gate_ref, o_ref, *, glu):
    r = _dot(a_ref[...], w_ref[...])
    if glu:
        half = r.shape[1] // 2
        r = r[:, :half] * _sigmoid(r[:, half:])
    o_ref[...] = x_ref[...] + gate_ref[0] * r


def _matmul_res(a, w, x, gate, seq, glu, a_time_major=False, tm_pref=512):
    t, k = a.shape
    n = w.shape[1]
    d = x.shape[1]
    tm = _row_tile(min(t, seq) if seq > 1 else t, tm_pref)
    if a_time_major:
        per = seq // tm
        a = a.reshape(seq, (t // seq) * k)
        a_spec = pl.BlockSpec((tm, k), lambda i: (i % per, i // per))
    else:
        a_spec = pl.BlockSpec((tm, k), lambda i: (i, 0))
    return pl.pallas_call(
        functools.partial(_matmul_res_kernel, glu=glu),
        out_shape=jax.ShapeDtypeStruct((t, d), F32),
        grid=(t // tm,),
        in_specs=[
            a_spec,
            pl.BlockSpec((k, n), lambda i: (0, 0)),
            pl.BlockSpec((tm, d), lambda i: (i, 0)),
            _mod_specs(tm, d, seq),
        ],
        out_specs=pl.BlockSpec((tm, d), lambda i: (i, 0)),
        compiler_params=_cparams("parallel"),
        name="matmul_residual",
    )(a, w, x, gate)


def _rotary(x, cos, sin):
    half = x.shape[-1] // 2
    x1 = x[:, :half]
    x2 = x[:, half:]
    return jnp.concatenate([x1 * cos - x2 * sin, x2 * cos + x1 * sin], axis=-1)


def _group_norm_gate(o, g):
    mu = jnp.mean(o, axis=-1, keepdims=True)
    var = jnp.mean(jnp.square(o - mu), axis=-1, keepdims=True)
    on = (o - mu) * lax.rsqrt(var + EPS)
    return (g * _sigmoid(g)) * on


def _ret_chunk_kernel(q_ref, k_ref, v_ref, g_ref, cos_ref, sin_ref, dmask_ref, cross_ref, tail_ref,
                      cdec_ref, o_ref, sfin_ref, s_scr, *, n_sub, chunk, k_scale):
    c = pl.program_id(2)

    @pl.when(c == 0)
    def _():
        s_scr[...] = jnp.zeros_like(s_scr)

    dmask = dmask_ref[0]
    cross = cross_ref[0]
    tail = tail_ref[0]
    cdec = cdec_ref[0]
    for j in range(n_sub):
        rows = pl.ds(j * chunk, chunk)
        cos = cos_ref[rows, :]
        sin = sin_ref[rows, :]
        qc = _rotary(q_ref[rows, :], cos, sin)
        kc = _rotary(k_ref[rows, :], cos, sin) * k_scale
        vb = v_ref[rows, :].astype(BF16)
        qb = qc.astype(BF16)
        att = _dot_nt(qb, kc.astype(BF16)) * dmask
        s = s_scr[...]
        o = _dot(att.astype(BF16), vb) + _dot(qb, s.astype(BF16)) * cross
        s_scr[...] = s * cdec + _dot_tn((kc * tail).astype(BF16), vb)
        o_ref[rows, :] = _group_norm_gate(o, g_ref[rows, :]).astype(BF16)

    @pl.when(c == pl.num_programs(2) - 1)
    def _():
        sfin_ref[0, 0] = s_scr[...]


def _ret_tables(heads, chunk):
    lg = jnp.log1p(-jnp.exp2(-5.0 - jnp.arange(heads, dtype=F32)))
    i = jnp.arange(chunk, dtype=F32)
    diff = i[:, None] - i[None, :]
    dmask = jnp.where(diff >= 0, jnp.exp(lg[:, None, None] * jnp.maximum(diff, 0.0)), 0.0)
    cross = jnp.exp(lg[:, None] * (i[None, :] + 1.0))[:, :, None]
    tail = jnp.exp(lg[:, None] * (chunk - 1.0 - i[None, :]))[:, :, None]
    cdec = jnp.exp(lg * chunk)
    return dmask, cross, tail, cdec


def _rope_tables(pos, half):
    inv = ROPE_BASE ** (-jnp.arange(half, dtype=F32) / half)
    ang = pos.astype(F32)[:, None] * inv[None, :]
    return jnp.cos(ang), jnp.sin(ang)


def _retention_prompt(proj, batch, seq, d):
    heads = RET_HEADS
    dk = d // heads
    dv = 2 * dk
    chunk = RET_CHUNK
    rows = _row_tile(seq, 4 * chunk)
    n_sub = rows // chunk
    nblk = seq // rows
    cos, sin = _rope_tables(jnp.arange(seq, dtype=jnp.int32), dk // 2)
    dmask, cross, tail, cdec = _ret_tables(heads, chunk)
    cdec = jnp.broadcast_to(cdec[:, None, None], (heads, 1, dv))
    kq = heads
    vq = (2 * heads * dk) // dv
    gq = vq + heads
    o, sfin = pl.pallas_call(
        functools.partial(_ret_chunk_kernel, n_sub=n_sub, chunk=chunk, k_scale=dk ** -0.5),
        out_shape=[jax.ShapeDtypeStruct((batch * seq, heads * dv), BF16),
                   jax.ShapeDtypeStruct((batch, heads, dk, dv), F32)],
        grid=(batch, heads, nblk),
        in_specs=[
            pl.BlockSpec((rows, dk), lambda b, h, c: (b * nblk + c, h)),
            pl.BlockSpec((rows, dk), lambda b, h, c: (b * nblk + c, kq + h)),
            pl.BlockSpec((rows, dv), lambda b, h, c: (b * nblk + c, vq + h)),
            pl.BlockSpec((rows, dv), lambda b, h, c: (b * nblk + c, gq + h)),
            pl.BlockSpec((rows, dk // 2), lambda b, h, c: (c, 0)),
            pl.BlockSpec((rows, dk // 2), lambda b, h, c: (c, 0)),
            pl.BlockSpec((1, chunk, chunk), lambda b, h, c: (h, 0, 0)),
            pl.BlockSpec((1, chunk, 1), lambda b, h, c: (h, 0, 0)),
            pl.BlockSpec((1, chunk, 1), lambda b, h, c: (h, 0, 0)),
            pl.BlockSpec((1, 1, dv), lambda b, h, c: (h, 0, 0)),
        ],
        out_specs=[
            pl.BlockSpec((rows, dv), lambda b, h, c: (b * nblk + c, h)),
            pl.BlockSpec((1, 1, dk, dv), lambda b, h, c: (b, h, 0, 0)),
        ],
        scratch_shapes=[pltpu.VMEM((dk, dv), F32)],
        compiler_params=_cparams("parallel", "parallel", "arbitrary"),
        name="retention_chunks",
    )(proj, proj, proj, proj, cos, sin, dmask, cross, tail, cdec)
    return o, sfin


def _ret_step_kernel(q_ref, k_ref, v_ref, g_ref, cos_ref, sin_ref, gam_ref, s0_ref, *rest, nb, k_scale):
    o_ref, s_ref = rest[-2:]
    cos = cos_ref[...]
    sin = sin_ref[...]
    gam = gam_ref[0]
    qc = _rotary(q_ref[...], cos, sin)
    kc = _rotary(k_ref[...], cos, sin) * k_scale
    qr = qc.astype(BF16).astype(F32)
    kr = kc.astype(BF16).astype(F32)
    vr = v_ref[...].astype(BF16).astype(F32)
    att = jnp.sum(qr * kr, axis=-1, keepdims=True)
    att = att.astype(BF16).astype(F32)
    dk = qc.shape[1]
    eye = (lax.broadcasted_iota(jnp.int32, (dk, dk), 0) == lax.broadcasted_iota(jnp.int32, (dk, dk), 1))
    rows8 = lax.broadcasted_iota(jnp.int32, (V7X_SUBLANES, dk), 0)
    o = jnp.zeros(vr.shape, F32)
    brow = lax.broadcasted_iota(jnp.int32, vr.shape, 0)
    for b in range(nb):
        s0 = s0_ref[0, b, 0]
        qrow = qc[b:b + 1, :]
        q8 = jnp.where(rows8 == 0, qrow, 0.0).astype(BF16)
        qs = _dot(q8, s0.astype(BF16))[0:1, :]
        o = jnp.where(brow == b, att[b:b + 1, :] * vr[b:b + 1, :] + qs * gam, o)
        kcol = jnp.sum(jnp.where(eye, kr[b:b + 1, :], 0.0), axis=1, keepdims=True)
        s_ref[0, b, 0] = s0 * gam + kcol * vr[b:b + 1, :]
    o_ref[...] = _group_norm_gate(o, g_ref[...]).astype(BF16)


def _retention_step(proj, states, layer, new_states, d):
    t = proj.shape[0]
    heads = RET_HEADS
    dk = d // heads
    dv = 2 * dk
    nb = V7X_SUBLANES
    cos, sin = _rope_tables(PAST_LEN + jnp.arange(1, dtype=jnp.int32), dk // 2)
    lg = jnp.log1p(-jnp.exp2(-5.0 - jnp.arange(heads, dtype=F32)))
    gam = jnp.broadcast_to(jnp.exp(lg)[:, None, None], (heads, 1, dv))
    kq = heads
    vq = (2 * heads * dk) // dv
    gq = vq + heads
    state_spec = pl.BlockSpec((1, nb, 1, dk, dv), lambda i, h: (layer, i, h, 0, 0))
    in_specs = [
        pl.BlockSpec((nb, dk), lambda i, h: (i, h)),
        pl.BlockSpec((nb, dk), lambda i, h: (i, kq + h)),
        pl.BlockSpec((nb, dv), lambda i, h: (i, vq + h)),
        pl.BlockSpec((nb, dv), lambda i, h: (i, gq + h)),
        pl.BlockSpec((1, dk // 2), lambda i, h: (0, 0)),
        pl.BlockSpec((1, dk // 2), lambda i, h: (0, 0)),
        pl.BlockSpec((1, 1, dv), lambda i, h: (h, 0, 0)),
        state_spec,
    ]
    args = [proj, proj, proj, proj, cos, sin, gam, states]
    aliases = {}
    if new_states is not None:
        in_specs.append(pl.BlockSpec(memory_space=pl.ANY))
        args.append(new_states)
        aliases = {len(args) - 1: 1}
    o, s = pl.pallas_call(
        functools.partial(_ret_step_kernel, nb=nb, k_scale=dk ** -0.5),
        out_shape=[jax.ShapeDtypeStruct((t, heads * dv), BF16),
                   jax.ShapeDtypeStruct(states.shape, F32)],
        grid=(t // nb, heads),
        in_specs=in_specs,
        out_specs=[pl.BlockSpec((nb, dv), lambda i, h: (i, h)), state_spec],
        input_output_aliases=aliases,
        compiler_params=_cparams("parallel", "parallel"),
        name="retention_step",
    )(*args)
    return o, s


def _ssm_disc_kernel(lr_ref, li_ref, ldt_ref, br_ref, bi_ref, are_ref, aim_ref, bbr_ref, bbi_ref):
    lr = lr_ref[...]
    li = li_ref[...]
    dt = jnp.exp(ldt_ref[...])
    mag = jnp.exp(lr * dt)
    a_re = mag * jnp.cos(li * dt)
    a_im = mag * jnp.sin(li * dt)
    den = lr * lr + li * li
    nr = a_re - 1.0
    c_re = (nr * lr + a_im * li) / den
    c_im = (a_im * lr - nr * li) / den
    are_ref[...] = a_re
    aim_ref[...] = a_im
    br = br_ref[...]
    bi = bi_ref[...]
    bbr_ref[...] = c_re * br - c_im * bi
    bbi_ref[...] = c_re * bi + c_im * br


def _ssm_discretize(lam_re, lam_im, log_dt, b_re, b_im):
    g, n = lam_re.shape
    p = b_re.shape[2]
    brt = jnp.transpose(b_re, (0, 2, 1))
    bit = jnp.transpose(b_im, (0, 2, 1))
    a_re, a_im, bbr, bbi = pl.pallas_call(
        _ssm_disc_kernel,
        out_shape=[jax.ShapeDtypeStruct((g, 1, n), F32), jax.ShapeDtypeStruct((g, 1, n), F32),
                   jax.ShapeDtypeStruct((g, p, n), F32), jax.ShapeDtypeStruct((g, p, n), F32)],
        name="ssm_discretize",
    )(lam_re.reshape(g, 1, n), lam_im.reshape(g, 1, n), log_dt.reshape(g, 1, 1), brt, bit)
    return a_re.reshape(g, n), a_im.reshape(g, n), bbr, bbi


def _block_diag(w, per):
    g, a, b = w.shape
    w = w.reshape(g // per, per, a, b)
    eye = jnp.eye(per, dtype=w.dtype)
    return jnp.einsum('cgab,gh->cgahb', w, eye).reshape(g // per, per * a, per * b)


def _ssm_bu(ub, bdr_ref, bdi_ref, nblk, kw):
    bur = jnp.concatenate([_dot(ub[:, c * kw:(c + 1) * kw], bdr_ref[c]) for c in range(nblk)], axis=1)
    bui = jnp.concatenate([_dot(ub[:, c * kw:(c + 1) * kw], bdi_ref[c]) for c in range(nblk)], axis=1)
    return bur, bui


def _ssm_y(hr, hi, cdr_ref, cdi_ref, nblk, kw):
    hrb = hr.astype(BF16)
    hib = hi.astype(BF16)
    return jnp.concatenate(
        [_dot(hrb[:, c * kw:(c + 1) * kw], cdr_ref[c]) - _dot(hib[:, c * kw:(c + 1) * kw], cdi_ref[c])
         for c in range(nblk)], axis=1)


def _ssm_scan_kernel(u_ref, bdr_ref, bdi_ref, cdr_ref, cdi_ref, d_ref, ar_ref, ai_ref,
                     z_ref, fr_ref, fi_ref, hr_scr, hi_scr, cr_scr, ci_scr, *, nblk, nb, lane_chunk):
    step = pl.program_id(0)

    @pl.when(step == 0)
    def _():
        cr_scr[...] = jnp.zeros_like(cr_scr)
        ci_scr[...] = jnp.zeros_like(ci_scr)

    u = u_ref[...]
    kw_in = u.shape[1] // nblk
    bur, bui = _ssm_bu(u.astype(BF16), bdr_ref, bdi_ref, nblk, kw_in)
    hr_scr[...] = bur
    hi_scr[...] = bui

    rows_total, gn = hr_scr.shape
    for c in range(gn // lane_chunk):
        lanes = pl.ds(c * lane_chunk, lane_chunk)
        ar = jnp.broadcast_to(ar_ref[:, lanes], (nb, lane_chunk))
        ai = jnp.broadcast_to(ai_ref[:, lanes], (nb, lane_chunk))

        def time_step(s, carry):
            hr, hi = carry
            rows = pl.ds(pl.multiple_of(s * nb, nb), nb)
            nr = hr_scr[rows, lanes] + (ar * hr - ai * hi)
            ni = hi_scr[rows, lanes] + (ar * hi + ai * hr)
            hr_scr[rows, lanes] = nr
            hi_scr[rows, lanes] = ni
            return nr, ni

        hr, hi = lax.fori_loop(0, rows_total // nb, time_step, (cr_scr[:, lanes], ci_scr[:, lanes]))
        cr_scr[:, lanes] = hr
        ci_scr[:, lanes] = hi

    kw_out = gn // nblk
    y = _ssm_y(hr_scr[...], hi_scr[...], cdr_ref, cdi_ref, nblk, kw_out) + d_ref[...] * u
    z_ref[...] = _gelu_tanh(y).astype(BF16)

    @pl.when(step == pl.num_programs(0) - 1)
    def _():
        fr_ref[...] = cr_scr[...]
        fi_ref[...] = ci_scr[...]


def _ssm_prompt(u_tm, disc, batch, seq):
    t, d = u_tm.shape
    bdr, bdi, cdr, cdi, dvec, a_re, a_im = disc
    nblk = bdr.shape[0]
    gn = a_re.shape[1]
    assert batch == V7X_SUBLANES, "one time step of all sequences must fill the sublanes of a vreg"
    rows = batch * _row_tile(seq, 32)
    full = lambda a: pl.BlockSpec(a.shape, lambda s: (0,) * a.ndim)
    z, fr, fi = pl.pallas_call(
        functools.partial(_ssm_scan_kernel, nblk=nblk, nb=batch, lane_chunk=4 * V7X_LANES),
        out_shape=[jax.ShapeDtypeStruct((t, d), BF16),
                   jax.ShapeDtypeStruct((batch, gn), F32),
                   jax.ShapeDtypeStruct((batch, gn), F32)],
        grid=(t // rows,),
        in_specs=[pl.BlockSpec((rows, d), lambda s: (s, 0)),
                  full(bdr), full(bdi), full(cdr), full(cdi), full(dvec), full(a_re), full(a_im)],
        out_specs=[pl.BlockSpec((rows, d), lambda s: (s, 0)),
                   pl.BlockSpec((batch, gn), lambda s: (0, 0)),
                   pl.BlockSpec((batch, gn), lambda s: (0, 0))],
        scratch_shapes=[pltpu.VMEM((rows, gn), F32), pltpu.VMEM((rows, gn), F32),
                        pltpu.VMEM((batch, gn), F32), pltpu.VMEM((batch, gn), F32)],
        compiler_params=_cparams("arbitrary"),
        name="ssm_scan",
    )(u_tm, bdr, bdi, cdr, cdi, dvec, a_re, a_im)
    return z, fr, fi


def _ssm_step_kernel(u_ref, h0r_ref, h0i_ref, bdr_ref, bdi_ref, cdr_ref, cdi_ref, d_ref, ar_ref, ai_ref,
                     z_ref, fr_ref, fi_ref, *, nblk):
    u = u_ref[...]
    kw_in = u.shape[1] // nblk
    bur, bui = _ssm_bu(u.astype(BF16), bdr_ref, bdi_ref, nblk, kw_in)
    ar = ar_ref[...]
    ai = ai_ref[...]
    h0r = h0r_ref[...]
    h0i = h0i_ref[...]
    hr = bur + (ar * h0r - ai * h0i)
    hi = bui + (ar * h0i + ai * h0r)
    fr_ref[...] = hr
    fi_ref[...] = hi
    kw_out = hr.shape[1] // nblk
    y = _ssm_y(hr, hi, cdr_ref, cdi_ref, nblk, kw_out) + d_ref[...] * u
    z_ref[...] = _gelu_tanh(y).astype(BF16)


def _ssm_step(u, h0r, h0i, disc):
    t, d = u.shape
    bdr, bdi, cdr, cdi, dvec, a_re, a_im = disc
    gn = a_re.shape[1]
    return pl.pallas_call(
        functools.partial(_ssm_step_kernel, nblk=bdr.shape[0]),
        out_shape=[jax.ShapeDtypeStruct((t, d), BF16),
                   jax.ShapeDtypeStruct((t, gn), F32), jax.ShapeDtypeStruct((t, gn), F32)],
        name="ssm_step",
        compiler_params=pltpu.CompilerParams(vmem_limit_bytes=V7X_VMEM_LIMIT_BYTES),
    )(u, h0r, h0i, bdr, bdi, cdr, cdi, dvec, a_re, a_im)


def _peer_scores_kernel(q_ref, key_ref, s_ref, *, nkh, nk, dq):
    q = q_ref[...].astype(BF16)
    for c in range(nkh):
        s_ref[c] = _dot_nt(key_ref[c], q[:, c * dq:(c + 1) * dq])


def _peer_scores(q, keys):
    t, n = q.shape
    nkh, nk, dq = keys.shape
    tm = _row_tile(t, 512)
    return pl.pallas_call(
        functools.partial(_peer_scores_kernel, nkh=nkh, nk=nk, dq=dq),
        out_shape=jax.ShapeDtypeStruct((nkh, nk, t), F32),
        grid=(t // tm,),
        in_specs=[pl.BlockSpec((tm, n), lambda i: (i, 0)),
                  pl.BlockSpec((nkh, nk, dq), lambda i: (0, 0, 0))],
        out_specs=pl.BlockSpec((nkh, nk, tm), lambda i: (0, 0, i)),
        compiler_params=_cparams("parallel"),
        name="peer_scores",
    )(q, keys)


def _remove_max_rounds(s, k, idx, want_round):
    cur = s
    rnd = jnp.full(s.shape, float(k), F32) if want_round else None
    vals = []
    for a in range(k):
        m = jnp.max(cur, axis=0, keepdims=True)
        if idx is None:
            hit = cur == m
        else:
            first = jnp.min(jnp.where(cur == m, idx, float(s.shape[0])), axis=0, keepdims=True)
            hit = idx == first
        if want_round:
            rnd = jnp.where(hit, float(a), rnd)
        cur = jnp.where(hit, NEG_INF, cur)
        vals.append(m)
    return vals, rnd, cur


def _rows_from_list(vals, lo, n, ridx):
    out = jnp.zeros(ridx.shape, F32)
    for r in range(n):
        out = jnp.where(ridx == float(r), vals[lo + r], out)
    return out


def _peer_select_one(s1, s2, topk, idx, cidx, valid, ridx_k, ridx_s):
    sub = V7X_SUBLANES
    stair = [topk // (a + 1) for a in range(topk)]
    v1, rank1, cur1 = _remove_max_rounds(s1, topk, idx, True)
    v2, rank2, cur2 = _remove_max_rounds(s2, topk, idx, True)
    v2all = _rows_from_list(v2, 0, topk, ridx_k)
    v2lo = _rows_from_list(v2, 0, sub, ridx_s)
    v1hi = _rows_from_list(v1, sub, topk - sub, ridx_s)
    blocks = [v1[0] + v2all] + [v1[a] + v2lo for a in range(1, sub)] + [v1hi + v2[0]]
    cand = jnp.where(valid, jnp.concatenate(blocks, axis=0), NEG_INF)
    tops, _, curc = _remove_max_rounds(cand, topk, cidx, False)
    taken = jnp.where(jnp.logical_and(curc == NEG_INF, valid), 1.0, 0.0)
    z = jnp.ones_like(tops[0])
    for m in tops[1:]:
        z = z + jnp.exp(m - tops[0])
    n1 = jnp.zeros(s1.shape, F32)
    total = jnp.zeros_like(z)
    for a in range(topk):
        if a == 0:
            n_a = jnp.sum(taken[0:topk, :], axis=0, keepdims=True)
        elif a < sub:
            base = topk + sub * (a - 1)
            n_a = jnp.sum(taken[base:base + sub, :], axis=0, keepdims=True)
        else:
            row = topk + sub * (sub - 1) + (a - sub)
            n_a = taken[row:row + 1, :]
        total = total + n_a
        n1 = jnp.where(rank1 == float(a), n_a, n1)
    if idx is None:
        k = float(topk)
        c1 = jnp.sum(jnp.where(cur1 == NEG_INF, 1.0, 0.0), axis=0, keepdims=True)
        c2 = jnp.sum(jnp.where(cur2 == NEG_INF, 1.0, 0.0), axis=0, keepdims=True)
        bad = jnp.where(jnp.logical_and(jnp.logical_and(c1 == k, c2 == k), total == k), 0.0, 1.0)
    else:
        bad = None
    e2 = jnp.exp(s2 - v2[0])
    e1 = jnp.exp(s1 - v1[0]) / z
    return rank2, e2, n1, e1, bad


def _peer_select_kernel(s_ref, r2_ref, e2_ref, n1_ref, e1_ref, *, heads, topk):
    sub = V7X_SUBLANES
    nk = s_ref.shape[1]
    lanes = s_ref.shape[2]
    ncand = topk + sub * sub
    idx = lax.broadcasted_iota(jnp.int32, (nk, lanes), 0).astype(F32)
    crow = lax.broadcasted_iota(jnp.int32, (ncand, lanes), 0)
    cidx = crow.astype(F32)
    ridx_k = lax.broadcasted_iota(jnp.int32, (topk, lanes), 0).astype(F32)
    ridx_s = lax.broadcasted_iota(jnp.int32, (sub, lanes), 0).astype(F32)
    valid = crow < topk + sub
    for a in range(2, sub):
        base = topk + sub * (a - 1)
        valid = jnp.logical_or(valid, jnp.logical_and(crow >= base, crow < base + topk // (a + 1)))
    valid = jnp.logical_or(valid, crow >= topk + sub * (sub - 1))

    def write(h, res):
        rank2, e2, n1, e1 = res
        r2_ref[h] = rank2.astype(r2_ref.dtype)
        e2_ref[h] = e2.astype(e2_ref.dtype)
        n1_ref[h] = n1
        e1_ref[h] = e1

    def head_pair(hp, carry):
        hs = (2 * hp, 2 * hp + 1)
        sc = [(s_ref[2 * h], s_ref[2 * h + 1]) for h in hs]
        bad = None
        for h, (s1, s2) in zip(hs, sc):
            *res, b = _peer_select_one(s1, s2, topk, None, None, valid, ridx_k, ridx_s)
            write(h, res)
            bad = b if bad is None else jnp.maximum(bad, b)

        @pl.when(jnp.max(bad) > 0.0)
        def _():
            for h, (s1, s2) in zip(hs, sc):
                *res, _ = _peer_select_one(s1, s2, topk, idx, cidx, valid, ridx_k, ridx_s)
                write(h, res)

        return carry

    lax.fori_loop(0, heads // 2, head_pair, 0)


def _peer_select(scores, heads, topk):
    nkh, nk, t = scores.shape
    tl = _row_tile(t, V7X_LANES)
    shp = jax.ShapeDtypeStruct((heads, nk, t), F32)
    shp_i2 = jax.ShapeDtypeStruct((heads, nk, t), BF16)
    ospec = pl.BlockSpec((heads, nk, tl), lambda i: (0, 0, i))
    return pl.pallas_call(
        functools.partial(_peer_select_kernel, heads=heads, topk=topk),
        out_shape=[shp_i2, shp_i2, shp, shp],
        grid=(t // tl,),
        in_specs=[pl.BlockSpec((nkh, nk, tl), lambda i: (0, 0, i))],
        out_specs=[ospec, ospec, ospec, ospec],
        compiler_params=_cparams("parallel"),
        name="peer_select",
    )(scores)


def _peer_dense_kernel(ht_ref, u_ref, vt_ref, r2_ref, e2_ref, n1_ref, e1_ref, x_ref, gate_ref, o_ref,
                       at0_scr, at1_scr, wt_scr, acc_scr, *, heads, nk, i1_per_step, nblk):
    j = pl.program_id(1)
    at_scr = (at0_scr, at1_scr)

    def activations(dst):
        dst[...] = _dot(u_ref[...], ht_ref[...])

    pack = 2 * V7X_SUBLANES
    tn = wt_scr.shape[1]

    def packed_row(ref, h, r):
        return jnp.broadcast_to(ref[h, r:r + 1, :], (pack, tn)).astype(BF16)[None]

    def mix(src):
        for r in range(i1_per_step):
            rows = pl.ds(r * nk, nk)
            g = None
            for h in range(heads):
                n = packed_row(n1_ref, h, r)
                e1 = packed_row(e1_ref, h, r)
                r2 = r2_ref[h].reshape(nk // pack, pack, tn)
                e2 = e2_ref[h].reshape(nk // pack, pack, tn)
                term = jnp.where(r2 < n, e2, jnp.zeros_like(e2)) * e1
                g = term if g is None else g + term
            g = g.reshape(nk, tn).astype(F32)
            wt_scr[rows, :] = (_gelu_tanh(src[rows, :]) * g).astype(BF16)
        acc_scr[...] += _dot(vt_ref[0], wt_scr[...])

    @pl.when(j == 0)
    def _():
        acc_scr[...] = jnp.zeros_like(acc_scr)
        activations(at_scr[0])

    for parity in range(2):
        @pl.when(jnp.logical_and(jnp.logical_and(j > 0, j < nblk), j % 2 == parity))
        def _():
            activations(at_scr[parity])
            mix(at_scr[1 - parity])

    @pl.when(j == nblk)
    def _():
        mix(at_scr[(nblk - 1) % 2])
        o_ref[...] = x_ref[...] + gate_ref[0] * jnp.transpose(acc_scr[...])


def _peer_dense(ht, u, vt, r2, e2, n1, e1, x, gate, seq):
    t, d = x.shape
    heads, nk, _ = r2.shape
    i1_per_step = V7X_SUBLANES
    nblk, _, te = vt.shape
    assert te == i1_per_step * nk and u.shape == (nblk * te, d)
    tn = _row_tile(min(t, seq) if seq > 1 else t, 512)
    clamp = lambda b: jnp.clip(b, 0, nblk - 1)
    tok = pl.BlockSpec((heads, nk, tn), lambda i, j: (0, 0, i))
    per_i1 = pl.BlockSpec((heads, i1_per_step, tn), lambda i, j: (0, clamp(j - 1), i))
    return pl.pallas_call(
        functools.partial(_peer_dense_kernel, heads=heads, nk=nk, i1_per_step=i1_per_step, nblk=nblk),
        out_shape=jax.ShapeDtypeStruct((t, d), F32),
        grid=(t // tn, nblk + 1),
        in_specs=[
            pl.BlockSpec((d, tn), lambda i, j: (0, i)),
            pl.BlockSpec((te, d), lambda i, j: (clamp(j), 0)),
            pl.BlockSpec((1, d, te), lambda i, j: (clamp(j - 1), 0, 0)),
            tok, tok, per_i1, per_i1,
            pl.BlockSpec((tn, d), lambda i, j: (i, 0)),
            _mod_specs(tn, d, seq),
        ],
        out_specs=pl.BlockSpec((tn, d), lambda i, j: (i, 0)),
        scratch_shapes=[pltpu.VMEM((te, tn), F32), pltpu.VMEM((te, tn), F32),
                        pltpu.VMEM((te, tn), BF16), pltpu.VMEM((d, tn), F32)],
        compiler_params=_cparams("parallel", "arbitrary"),
        name="peer_dense",
    )(ht, u, vt, r2, e2, n1, e1, x, gate)


def _rmsnorm_kernel(x_ref, g_ref, o_ref):
    x = x_ref[...]
    o_ref[...] = (x * lax.rsqrt(jnp.mean(x * x, axis=-1, keepdims=True) + EPS)) * g_ref[...]


def _rmsnorm(x, g):
    t, d = x.shape
    tm = _row_tile(t, 1024)
    return pl.pallas_call(
        _rmsnorm_kernel,
        out_shape=jax.ShapeDtypeStruct((t, d), F32),
        grid=(t // tm,),
        in_specs=[pl.BlockSpec((tm, d), lambda i: (i, 0)), pl.BlockSpec((1, d), lambda i: (0, 0))],
        out_specs=pl.BlockSpec((tm, d), lambda i: (i, 0)),
        compiler_params=_cparams("parallel"),
        name="final_rmsnorm",
    )(x, g)


def _mods(mod, lo, hi, seq):
    d = mod.shape[1] // 6
    rows = mod[lo:hi]
    parts = [rows[:, k * d:(k + 1) * d] for k in range(6)]
    if seq == 1:
        return [p[None, :, :] for p in parts]
    return [p[:, None, :] for p in parts]


def kernel(x_prompt, x_sample, c_prompt, c_sample, state_ret, state_ssm_re, state_ssm_im, norm_g, final_g, w_ada, b_ada, ret_w_in, ret_w_out, ssm_w_in, ssm_lam_re, ssm_lam_im, ssm_log_dt, ssm_b_re, ssm_b_im, ssm_c_re, ssm_c_im, ssm_d, ssm_w_glu, peer_w_q, peer_key1, peer_key2, peer_u, peer_v):
    batch, seq, d = x_prompt.shape
    dbatch, dseq, _ = x_sample.shape
    assert dseq == 1 and seq % RET_CHUNK == 0
    depth = w_ada.shape[0]
    heads = peer_key1.shape[1]

    c_all = jnp.concatenate([c_prompt, c_sample], axis=0)
    pad = (-c_all.shape[0]) % V7X_SUBLANES
    c_all = jnp.pad(c_all, ((0, pad), (0, 0)))
    mod = _ada(c_all, w_ada.astype(BF16), b_ada)

    groups = [
        dict(x=x_prompt.reshape(batch * seq, d), lo=0, hi=batch, seq=seq, nb=batch),
        dict(x=x_sample.reshape(dbatch, d), lo=batch, hi=batch + dbatch, seq=1, nb=dbatch),
    ]
    outs = [dict(ret=[], re=[], im=[]) for _ in groups]
    ret_sample = None

    for i in range(depth):
        jm = i // N_MIXERS
        is_ret = i % N_MIXERS == 0
        w_q = peer_w_q[i].astype(BF16)
        keys = jnp.stack([peer_key1[i], peer_key2[i]], axis=1).reshape(2 * heads, PEER_NKEYS, -1).astype(BF16)
        u_tab = peer_u[i].astype(BF16)
        te = V7X_SUBLANES * PEER_NKEYS
        vt_tab = jnp.transpose(peer_v[i].reshape(-1, te, d), (0, 2, 1)).astype(BF16)
        g_mix = norm_g[i, 0].reshape(1, d)
        g_peer = norm_g[i, 1].reshape(1, d)
        if is_ret:
            w_in = ret_w_in[jm].astype(BF16)
            w_out = ret_w_out[jm].astype(BF16)
        else:
            w_in = ssm_w_in[jm].astype(BF16)
            w_out = ssm_w_glu[jm].astype(BF16)
            a_re, a_im, bbr, bbi = _ssm_discretize(ssm_lam_re[jm], ssm_lam_im[jm], ssm_log_dt[jm],
                                                   ssm_b_re[jm], ssm_b_im[jm])
            per = V7X_MXU_DIM // SSM_GROUP
            disc = (_block_diag(bbr, per).astype(BF16), _block_diag(bbi, per).astype(BF16),
                    _block_diag(jnp.transpose(ssm_c_re[jm], (0, 2, 1)), per).astype(BF16),
                    _block_diag(jnp.transpose(ssm_c_im[jm], (0, 2, 1)), per).astype(BF16),
                    ssm_d[jm].reshape(1, d), a_re.reshape(1, -1), a_im.reshape(1, -1))

        for gi, grp in enumerate(groups):
            x = grp["x"]
            sq = grp["seq"]
            sh1, sc1, g1, sh2, sc2, g2 = _mods(mod[i], grp["lo"], grp["hi"], sq)
            tmaj = sq > 1 and not is_ret
            proj = _nm_matmul(x, g_mix, sh1, sc1, w_in, sq, time_major=tmaj)
            if is_ret:
                if sq > 1:
                    y, s = _retention_prompt(proj, grp["nb"], sq, d)
                    outs[gi]["ret"].append(s)
                else:
                    y, ret_sample = _retention_step(proj, state_ret, jm, ret_sample, d)
                x = _matmul_res(y, w_out, x, g1, sq, glu=False)
            else:
                if sq > 1:
                    z, fr, fi = _ssm_prompt(proj, disc, grp["nb"], sq)
                else:
                    z, fr, fi = _ssm_step(proj, state_ssm_re[jm].reshape(dbatch, -1),
                                          state_ssm_im[jm].reshape(dbatch, -1), disc)
                outs[gi]["re"].append(fr.reshape(grp["nb"], -1, SSM_STATE))
                outs[gi]["im"].append(fi.reshape(grp["nb"], -1, SSM_STATE))
                x = _matmul_res(z, w_out, x, g1, sq, glu=True, a_time_major=tmaj)
            q, h2 = _nm_matmul(x, g_peer, sh2, sc2, w_q, sq, emit_h=True)
            scores = _peer_scores(q, keys)
            r2, e2, n1, e1 = _peer_select(scores, heads, PEER_TOPK)
            x = _peer_dense(h2, u_tab, vt_tab, r2, e2, n1, e1, x, g2, sq)
            grp["x"] = x

    fg = final_g.reshape(1, d)
    y_prompt = _rmsnorm(groups[0]["x"], fg).reshape(batch, seq, d)
    y_sample = _rmsnorm(groups[1]["x"], fg).reshape(dbatch, 1, d)
    return (y_prompt, y_sample,
            jnp.stack(outs[0]["ret"]), ret_sample,
            jnp.stack(outs[0]["re"]), jnp.stack(outs[0]["im"]),
            jnp.stack(outs[1]["re"]), jnp.stack(outs[1]["im"]))
```

```python
import functools
import math

import jax
import jax.numpy as jnp
import numpy as np
from jax import lax
from jax.experimental import pallas as pl
from jax.experimental.pallas import tpu as pltpu

F32 = jnp.float32
BF16 = jnp.bfloat16

EPS = 1e-6
ROPE_BASE = 10000.0
PAST_LEN = 16384
RET_HEADS = 4
RET_CHUNK = 128
SSM_GROUP = 16
SSM_STATE = 64
PEER_HEADS = 8
PEER_NKEYS = 128
PEER_TOPK = 16
N_MIXERS = 2

V7X_LANES = 128
V7X_SUBLANES = 8
V7X_MXU_DIM = 256
V7X_VMEM_LIMIT_BYTES = 56 * 1024 * 1024

NEG_INF = float("-inf")


def _cparams(*sem):
    return pltpu.CompilerParams(dimension_semantics=sem, vmem_limit_bytes=V7X_VMEM_LIMIT_BYTES)


def _dot(a, b):
    return jnp.dot(a, b, preferred_element_type=F32)


def _dot_nt(a, b):
    return lax.dot_general(a, b, (((1,), (1,)), ((), ())), preferred_element_type=F32)


def _dot_tn(a, b):
    return lax.dot_general(a, b, (((0,), (0,)), ((), ())), preferred_element_type=F32)


def _sigmoid(x):
    return 1.0 / (1.0 + jnp.exp(-x))


def _gelu_tanh(x):
    c = math.sqrt(2.0 / math.pi)
    hx = 0.5 * x
    return hx + hx * jnp.tanh(x * (c + (c * 0.044715) * (x * x)))


def _row_tile(t, pref):
    if t <= pref:
        return t
    tile = pref
    while t % tile:
        tile //= 2
    return tile


def _ada_kernel(c_ref, w_ref, b_ref, o_ref):
    c = c_ref[...]
    sc = (c * _sigmoid(c)).astype(BF16)
    o_ref[0] = _dot(sc, w_ref[0]) + b_ref[0]


def _ada(c_all, w_ada, b_ada):
    m, d = c_all.shape
    depth, _, n = w_ada.shape
    tn = 1024
    return pl.pallas_call(
        _ada_kernel,
        out_shape=jax.ShapeDtypeStruct((depth, m, n), F32),
        grid=(depth, n // tn),
        in_specs=[
            pl.BlockSpec((m, d), lambda l, j: (0, 0)),
            pl.BlockSpec((1, d, tn), lambda l, j: (l, 0, j)),
            pl.BlockSpec((1, 1, tn), lambda l, j: (l, 0, j)),
        ],
        out_specs=pl.BlockSpec((1, m, tn), lambda l, j: (l, 0, j)),
        compiler_params=_cparams("parallel", "parallel"),
        name="ada_mod",
    )(c_all, w_ada, b_ada.reshape(depth, 1, n))


def _norm_mod(x, g, sh, sc):
    y = x * lax.rsqrt(jnp.mean(x * x, axis=-1, keepdims=True) + EPS)
    return (y * g) * (1.0 + sc) + sh


def _nm_matmul_kernel(x_ref, g_ref, sh_ref, sc_ref, w_ref, o_ref, h_scr):
    @pl.when(pl.program_id(1) == 0)
    def _():
        h_scr[...] = _norm_mod(x_ref[...], g_ref[...], sh_ref[0], sc_ref[0]).astype(BF16)

    o_ref[...] = _dot(h_scr[...], w_ref[...])


def _mod_specs(tm, d, seq):
    if seq == 1:
        return pl.BlockSpec((1, tm, d), lambda i, *_: (0, i, 0))
    per = seq // tm
    return pl.BlockSpec((1, 1, d), lambda i, *_: (i // per, 0, 0))


def _nm_matmul(x, g, sh, sc, w, seq, time_major=False):
    t, d = x.shape
    n = w.shape[1]
    tm = _row_tile(min(t, seq) if seq > 1 else t, 512)
    tn = min(n, 1024)
    if time_major:
        per, nj = seq // tm, n // tn
        out_shape = jax.ShapeDtypeStruct((seq, (t // seq) * n), F32)
        out_spec = pl.BlockSpec((tm, tn), lambda i, j: (i % per, (i // per) * nj + j))
    else:
        out_shape = jax.ShapeDtypeStruct((t, n), F32)
        out_spec = pl.BlockSpec((tm, tn), lambda i, j: (i, j))
    res = pl.pallas_call(
        _nm_matmul_kernel,
        out_shape=out_shape,
        grid=(t // tm, n // tn),
        in_specs=[
            pl.BlockSpec((tm, d), lambda i, j: (i, 0)),
            pl.BlockSpec((1, d), lambda i, j: (0, 0)),
            _mod_specs(tm, d, seq),
            _mod_specs(tm, d, seq),
            pl.BlockSpec((d, tn), lambda i, j: (0, j)),
        ],
        out_specs=out_spec,
        scratch_shapes=[pltpu.VMEM((tm, d), BF16)],
        compiler_params=_cparams("parallel", "arbitrary"),
        name="norm_mod_matmul",
    )(x, g, sh, sc, w)
    return res.reshape(t, n) if time_major else res


def _matmul_res_kernel(a_ref, w_ref, x_ref, gate_ref, o_ref, *, glu):
    r = _dot(a_ref[...], w_ref[...])
    if glu:
        half = r.shape[1] // 2
        r = r[:, :half] * _sigmoid(r[:, half:])
    o_ref[...] = x_ref[...] + gate_ref[0] * r


def _matmul_res(a, w, x, gate, seq, glu, a_time_major=False, tm_pref=512):
    t, k = a.shape
    n = w.shape[1]
    d = x.shape[1]
    tm = _row_tile(min(t, seq) if seq > 1 else t, tm_pref)
    if a_time_major:
        per = seq // tm
        a = a.reshape(seq, (t // seq) * k)
        a_spec = pl.BlockSpec((tm, k), lambda i: (i % per, i // per))
    else:
        a_spec = pl.BlockSpec((tm, k), lambda i: (i, 0))
    return pl.pallas_call(
        functools.partial(_matmul_res_kernel, glu=glu),
        out_shape=jax.ShapeDtypeStruct((t, d), F32),
        grid=(t // tm,),
        in_specs=[
            a_spec,
            pl.BlockSpec((k, n), lambda i: (0, 0)),
            pl.BlockSpec((tm, d), lambda i: (i, 0)),
            _mod_specs(tm, d, seq),
        ],
        out_specs=pl.BlockSpec((tm, d), lambda i: (i, 0)),
        compiler_params=_cparams("parallel"),
        name="matmul_residual",
    )(a, w, x, gate)


def _rotary(x, cos, sin):
    half = x.shape[-1] // 2
    x1 = x[:, :half]
    x2 = x[:, half:]
    return jnp.concatenate([x1 * cos - x2 * sin, x2 * cos + x1 * sin], axis=-1)


def _group_norm_gate(o, g):
    mu = jnp.mean(o, axis=-1, keepdims=True)
    var = jnp.mean(jnp.square(o - mu), axis=-1, keepdims=True)
    on = (o - mu) * lax.rsqrt(var + EPS)
    return (g * _sigmoid(g)) * on


def _ret_chunk_kernel(q_ref, k_ref, v_ref, g_ref, cos_ref, sin_ref, dmask_ref, cross_ref, tail_ref,
                      cdec_ref, o_ref, sfin_ref, s_scr, *, n_sub, chunk, k_scale):
    c = pl.program_id(2)

    @pl.when(c == 0)
    def _():
        s_scr[...] = jnp.zeros_like(s_scr)

    dmask = dmask_ref[0]
    cross = cross_ref[0]
    tail = tail_ref[0]
    cdec = cdec_ref[0]
    for j in range(n_sub):
        rows = pl.ds(j * chunk, chunk)
        cos = cos_ref[rows, :]
        sin = sin_ref[rows, :]
        qc = _rotary(q_ref[rows, :], cos, sin)
        kc = _rotary(k_ref[rows, :], cos, sin) * k_scale
        vb = v_ref[rows, :].astype(BF16)
        qb = qc.astype(BF16)
        att = _dot_nt(qb, kc.astype(BF16)) * dmask
        s = s_scr[...]
        o = _dot(att.astype(BF16), vb) + _dot(qb, s.astype(BF16)) * cross
        s_scr[...] = s * cdec + _dot_tn((kc * tail).astype(BF16), vb)
        o_ref[rows, :] = _group_norm_gate(o, g_ref[rows, :]).astype(BF16)

    @pl.when(c == pl.num_programs(2) - 1)
    def _():
        sfin_ref[0, 0] = s_scr[...]


def _ret_tables(heads, chunk):
    lg = jnp.log1p(-jnp.exp2(-5.0 - jnp.arange(heads, dtype=F32)))
    i = jnp.arange(chunk, dtype=F32)
    diff = i[:, None] - i[None, :]
    dmask = jnp.where(diff >= 0, jnp.exp(lg[:, None, None] * jnp.maximum(diff, 0.0)), 0.0)
    cross = jnp.exp(lg[:, None] * (i[None, :] + 1.0))[:, :, None]
    tail = jnp.exp(lg[:, None] * (chunk - 1.0 - i[None, :]))[:, :, None]
    cdec = jnp.exp(lg * chunk)
    return dmask, cross, tail, cdec


def _rope_tables(pos, half):
    inv = ROPE_BASE ** (-jnp.arange(half, dtype=F32) / half)
    ang = pos.astype(F32)[:, None] * inv[None, :]
    return jnp.cos(ang), jnp.sin(ang)


def _retention_prompt(proj, batch, seq, d):
    heads = RET_HEADS
    dk = d // heads
    dv = 2 * dk
    chunk = RET_CHUNK
    rows = _row_tile(seq, 4 * chunk)
    n_sub = rows // chunk
    nblk = seq // rows
    cos, sin = _rope_tables(jnp.arange(seq, dtype=jnp.int32), dk // 2)
    dmask, cross, tail, cdec = _ret_tables(heads, chunk)
    cdec = jnp.broadcast_to(cdec[:, None, None], (heads, 1, dv))
    kq = heads
    vq = (2 * heads * dk) // dv
    gq = vq + heads
    o, sfin = pl.pallas_call(
        functools.partial(_ret_chunk_kernel, n_sub=n_sub, chunk=chunk, k_scale=dk ** -0.5),
        out_shape=[jax.ShapeDtypeStruct((batch * seq, heads * dv), BF16),
                   jax.ShapeDtypeStruct((batch, heads, dk, dv), F32)],
        grid=(batch, heads, nblk),
        in_specs=[
            pl.BlockSpec((rows, dk), lambda b, h, c: (b * nblk + c, h)),
            pl.BlockSpec((rows, dk), lambda b, h, c: (b * nblk + c, kq + h)),
            pl.BlockSpec((rows, dv), lambda b, h, c: (b * nblk + c, vq + h)),
            pl.BlockSpec((rows, dv), lambda b, h, c: (b * nblk + c, gq + h)),
            pl.BlockSpec((rows, dk // 2), lambda b, h, c: (c, 0)),
            pl.BlockSpec((rows, dk // 2), lambda b, h, c: (c, 0)),
            pl.BlockSpec((1, chunk, chunk), lambda b, h, c: (h, 0, 0)),
            pl.BlockSpec((1, chunk, 1), lambda b, h, c: (h, 0, 0)),
            pl.BlockSpec((1, chunk, 1), lambda b, h, c: (h, 0, 0)),
            pl.BlockSpec((1, 1, dv), lambda b, h, c: (h, 0, 0)),
        ],
        out_specs=[
            pl.BlockSpec((rows, dv), lambda b, h, c: (b * nblk + c, h)),
            pl.BlockSpec((1, 1, dk, dv), lambda b, h, c: (b, h, 0, 0)),
        ],
        scratch_shapes=[pltpu.VMEM((dk, dv), F32)],
        compiler_params=_cparams("parallel", "parallel", "arbitrary"),
        name="retention_chunks",
    )(proj, proj, proj, proj, cos, sin, dmask, cross, tail, cdec)
    return o, sfin


def _ret_step_kernel(q_ref, k_ref, v_ref, g_ref, cos_ref, sin_ref, gam_ref, s0_ref, *rest, nb, k_scale):
    o_ref, s_ref = rest[-2:]
    cos = cos_ref[...]
    sin = sin_ref[...]
    gam = gam_ref[0]
    qc = _rotary(q_ref[...], cos, sin)
    kc = _rotary(k_ref[...], cos, sin) * k_scale
    qr = qc.astype(BF16).astype(F32)
    kr = kc.astype(BF16).astype(F32)
    vr = v_ref[...].astype(BF16).astype(F32)
    att = jnp.sum(qr * kr, axis=-1, keepdims=True)
    att = att.astype(BF16).astype(F32)
    dk = qc.shape[1]
    eye = (lax.broadcasted_iota(jnp.int32, (dk, dk), 0) == lax.broadcasted_iota(jnp.int32, (dk, dk), 1))
    rows8 = lax.broadcasted_iota(jnp.int32, (V7X_SUBLANES, dk), 0)
    o = jnp.zeros(vr.shape, F32)
    brow = lax.broadcasted_iota(jnp.int32, vr.shape, 0)
    for b in range(nb):
        s0 = s0_ref[0, b, 0]
        qrow = qc[b:b + 1, :]
        q8 = jnp.where(rows8 == 0, qrow, 0.0).astype(BF16)
        qs = _dot(q8, s0.astype(BF16))[0:1, :]
        o = jnp.where(brow == b, att[b:b + 1, :] * vr[b:b + 1, :] + qs * gam, o)
        kcol = jnp.sum(jnp.where(eye, kr[b:b + 1, :], 0.0), axis=1, keepdims=True)
        s_ref[0, b, 0] = s0 * gam + kcol * vr[b:b + 1, :]
    o_ref[...] = _group_norm_gate(o, g_ref[...]).astype(BF16)


def _retention_step(proj, states, layer, new_states, d):
    t = proj.shape[0]
    heads = RET_HEADS
    dk = d // heads
    dv = 2 * dk
    nb = V7X_SUBLANES
    cos, sin = _rope_tables(PAST_LEN + jnp.arange(1, dtype=jnp.int32), dk // 2)
    lg = jnp.log1p(-jnp.exp2(-5.0 - jnp.arange(heads, dtype=F32)))
    gam = jnp.broadcast_to(jnp.exp(lg)[:, None, None], (heads, 1, dv))
    kq = heads
    vq = (2 * heads * dk) // dv
    gq = vq + heads
    state_spec = pl.BlockSpec((1, nb, 1, dk, dv), lambda i, h: (layer, i, h, 0, 0))
    in_specs = [
        pl.BlockSpec((nb, dk), lambda i, h: (i, h)),
        pl.BlockSpec((nb, dk), lambda i, h: (i, kq + h)),
        pl.BlockSpec((nb, dv), lambda i, h: (i, vq + h)),
        pl.BlockSpec((nb, dv), lambda i, h: (i, gq + h)),
        pl.BlockSpec((1, dk // 2), lambda i, h: (0, 0)),
        pl.BlockSpec((1, dk // 2), lambda i, h: (0, 0)),
        pl.BlockSpec((1, 1, dv), lambda i, h: (h, 0, 0)),
        state_spec,
    ]
    args = [proj, proj, proj, proj, cos, sin, gam, states]
    aliases = {}
    if new_states is not None:
        in_specs.append(pl.BlockSpec(memory_space=pl.ANY))
        args.append(new_states)
        aliases = {len(args) - 1: 1}
    o, s = pl.pallas_call(
        functools.partial(_ret_step_kernel, nb=nb, k_scale=dk ** -0.5),
        out_shape=[jax.ShapeDtypeStruct((t, heads * dv), BF16),
                   jax.ShapeDtypeStruct(states.shape, F32)],
        grid=(t // nb, heads),
        in_specs=in_specs,
        out_specs=[pl.BlockSpec((nb, dv), lambda i, h: (i, h)), state_spec],
        input_output_aliases=aliases,
        compiler_params=_cparams("parallel", "parallel"),
        name="retention_step",
    )(*args)
    return o, s


def _ssm_disc_kernel(lr_ref, li_ref, ldt_ref, br_ref, bi_ref, are_ref, aim_ref, bbr_ref, bbi_ref):
    lr = lr_ref[...]
    li = li_ref[...]
    dt = jnp.exp(ldt_ref[...])
    mag = jnp.exp(lr * dt)
    a_re = mag * jnp.cos(li * dt)
    a_im = mag * jnp.sin(li * dt)
    den = lr * lr + li * li
    nr = a_re - 1.0
    c_re = (nr * lr + a_im * li) / den
    c_im = (a_im * lr - nr * li) / den
    are_ref[...] = a_re
    aim_ref[...] = a_im
    br = br_ref[...]
    bi = bi_ref[...]
    bbr_ref[...] = c_re * br - c_im * bi
    bbi_ref[...] = c_re * bi + c_im * br


def _ssm_discretize(lam_re, lam_im, log_dt, b_re, b_im):
    g, n = lam_re.shape
    p = b_re.shape[2]
    brt = jnp.transpose(b_re, (0, 2, 1))
    bit = jnp.transpose(b_im, (0, 2, 1))
    a_re, a_im, bbr, bbi = pl.pallas_call(
        _ssm_disc_kernel,
        out_shape=[jax.ShapeDtypeStruct((g, 1, n), F32), jax.ShapeDtypeStruct((g, 1, n), F32),
                   jax.ShapeDtypeStruct((g, p, n), F32), jax.ShapeDtypeStruct((g, p, n), F32)],
        name="ssm_discretize",
    )(lam_re.reshape(g, 1, n), lam_im.reshape(g, 1, n), log_dt.reshape(g, 1, 1), brt, bit)
    return a_re.reshape(g, n), a_im.reshape(g, n), bbr, bbi


def _block_diag(w, per):
    g, a, b = w.shape
    w = w.reshape(g // per, per, a, b)
    eye = jnp.eye(per, dtype=w.dtype)
    return jnp.einsum('cgab,gh->cgahb', w, eye).reshape(g // per, per * a, per * b)


def _ssm_bu(ub, bdr_ref, bdi_ref, nblk, kw):
    bur = jnp.concatenate([_dot(ub[:, c * kw:(c + 1) * kw], bdr_ref[c]) for c in range(nblk)], axis=1)
    bui = jnp.concatenate([_dot(ub[:, c * kw:(c + 1) * kw], bdi_ref[c]) for c in range(nblk)], axis=1)
    return bur, bui


def _ssm_y(hr, hi, cdr_ref, cdi_ref, nblk, kw):
    hrb = hr.astype(BF16)
    hib = hi.astype(BF16)
    return jnp.concatenate(
        [_dot(hrb[:, c * kw:(c + 1) * kw], cdr_ref[c]) - _dot(hib[:, c * kw:(c + 1) * kw], cdi_ref[c])
         for c in range(nblk)], axis=1)


def _ssm_scan_kernel(u_ref, bdr_ref, bdi_ref, cdr_ref, cdi_ref, d_ref, ar_ref, ai_ref,
                     z_ref, fr_ref, fi_ref, hr_scr, hi_scr, cr_scr, ci_scr, *, nblk, nb, lane_chunk):
    step = pl.program_id(0)

    @pl.when(step == 0)
    def _():
        cr_scr[...] = jnp.zeros_like(cr_scr)
        ci_scr[...] = jnp.zeros_like(ci_scr)

    u = u_ref[...]
    kw_in = u.shape[1] // nblk
    bur, bui = _ssm_bu(u.astype(BF16), bdr_ref, bdi_ref, nblk, kw_in)
    hr_scr[...] = bur
    hi_scr[...] = bui

    rows_total, gn = hr_scr.shape
    for c in range(gn // lane_chunk):
        lanes = pl.ds(c * lane_chunk, lane_chunk)
        ar = jnp.broadcast_to(ar_ref[:, lanes], (nb, lane_chunk))
        ai = jnp.broadcast_to(ai_ref[:, lanes], (nb, lane_chunk))

        def time_step(s, carry):
            hr, hi = carry
            rows = pl.ds(pl.multiple_of(s * nb, nb), nb)
            nr = hr_scr[rows, lanes] + (ar * hr - ai * hi)
            ni = hi_scr[rows, lanes] + (ar * hi + ai * hr)
            hr_scr[rows, lanes] = nr
            hi_scr[rows, lanes] = ni
            return nr, ni

        hr, hi = lax.fori_loop(0, rows_total // nb, time_step, (cr_scr[:, lanes], ci_scr[:, lanes]))
        cr_scr[:, lanes] = hr
        ci_scr[:, lanes] = hi

    kw_out = gn // nblk
    y = _ssm_y(hr_scr[...], hi_scr[...], cdr_ref, cdi_ref, nblk, kw_out) + d_ref[...] * u
    z_ref[...] = _gelu_tanh(y).astype(BF16)

    @pl.when(step == pl.num_programs(0) - 1)
    def _():
        fr_ref[...] = cr_scr[...]
        fi_ref[...] = ci_scr[...]


def _ssm_prompt(u_tm, disc, batch, seq):
    t, d = u_tm.shape
    bdr, bdi, cdr, cdi, dvec, a_re, a_im = disc
    nblk = bdr.shape[0]
    gn = a_re.shape[1]
    assert batch == V7X_SUBLANES, "one time step of all sequences must fill the sublanes of a vreg"
    rows = batch * _row_tile(seq, 32)
    full = lambda a: pl.BlockSpec(a.shape, lambda s: (0,) * a.ndim)
    z, fr, fi = pl.pallas_call(
        functools.partial(_ssm_scan_kernel, nblk=nblk, nb=batch, lane_chunk=4 * V7X_LANES),
        out_shape=[jax.ShapeDtypeStruct((t, d), BF16),
                   jax.ShapeDtypeStruct((batch, gn), F32),
                   jax.ShapeDtypeStruct((batch, gn), F32)],
        grid=(t // rows,),
        in_specs=[pl.BlockSpec((rows, d), lambda s: (s, 0)),
                  full(bdr), full(bdi), full(cdr), full(cdi), full(dvec), full(a_re), full(a_im)],
        out_specs=[pl.BlockSpec((rows, d), lambda s: (s, 0)),
                   pl.BlockSpec((batch, gn), lambda s: (0, 0)),
                   pl.BlockSpec((batch, gn), lambda s: (0, 0))],
        scratch_shapes=[pltpu.VMEM((rows, gn), F32), pltpu.VMEM((rows, gn), F32),
                        pltpu.VMEM((batch, gn), F32), pltpu.VMEM((batch, gn), F32)],
        compiler_params=_cparams("arbitrary"),
        name="ssm_scan",
    )(u_tm, bdr, bdi, cdr, cdi, dvec, a_re, a_im)
    return z, fr, fi


def _ssm_step_kernel(u_ref, h0r_ref, h0i_ref, bdr_ref, bdi_ref, cdr_ref, cdi_ref, d_ref, ar_ref, ai_ref,
                     z_ref, fr_ref, fi_ref, *, nblk):
    u = u_ref[...]
    kw_in = u.shape[1] // nblk
    bur, bui = _ssm_bu(u.astype(BF16), bdr_ref, bdi_ref, nblk, kw_in)
    ar = ar_ref[...]
    ai = ai_ref[...]
    h0r = h0r_ref[...]
    h0i = h0i_ref[...]
    hr = bur + (ar * h0r - ai * h0i)
    hi = bui + (ar * h0i + ai * h0r)
    fr_ref[...] = hr
    fi_ref[...] = hi
    kw_out = hr.shape[1] // nblk
    y = _ssm_y(hr, hi, cdr_ref, cdi_ref, nblk, kw_out) + d_ref[...] * u
    z_ref[...] = _gelu_tanh(y).astype(BF16)


def _ssm_step(u, h0r, h0i, disc):
    t, d = u.shape
    bdr, bdi, cdr, cdi, dvec, a_re, a_im = disc
    gn = a_re.shape[1]
    return pl.pallas_call(
        functools.partial(_ssm_step_kernel, nblk=bdr.shape[0]),
        out_shape=[jax.ShapeDtypeStruct((t, d), BF16),
                   jax.ShapeDtypeStruct((t, gn), F32), jax.ShapeDtypeStruct((t, gn), F32)],
        name="ssm_step",
        compiler_params=pltpu.CompilerParams(vmem_limit_bytes=V7X_VMEM_LIMIT_BYTES),
    )(u, h0r, h0i, bdr, bdi, cdr, cdi, dvec, a_re, a_im)


def _peer_scores_kernel(x_ref, g_ref, sh_ref, sc_ref, w_ref, key_ref, s_ref, ht_ref, *, nkh, dq):
    h = _norm_mod(x_ref[...], g_ref[...], sh_ref[0], sc_ref[0])
    ht_ref[...] = jnp.transpose(h).astype(BF16)
    q = _dot(h.astype(BF16), w_ref[...]).astype(BF16)
    for c in range(nkh):
        s_ref[c] = _dot_nt(key_ref[c], q[:, c * dq:(c + 1) * dq])


def _peer_scores(x, g, sh, sc, w_q, keys, seq):
    t, d = x.shape
    n = w_q.shape[1]
    nkh, nk, dq = keys.shape
    tm = _row_tile(min(t, seq) if seq > 1 else t, 512)
    return pl.pallas_call(
        functools.partial(_peer_scores_kernel, nkh=nkh, dq=dq),
        out_shape=[jax.ShapeDtypeStruct((nkh, nk, t), F32), jax.ShapeDtypeStruct((d, t), BF16)],
        grid=(t // tm,),
        in_specs=[pl.BlockSpec((tm, d), lambda i: (i, 0)),
                  pl.BlockSpec((1, d), lambda i: (0, 0)),
                  _mod_specs(tm, d, seq),
                  _mod_specs(tm, d, seq),
                  pl.BlockSpec((d, n), lambda i: (0, 0)),
                  pl.BlockSpec((nkh, nk, dq), lambda i: (0, 0, 0))],
        out_specs=[pl.BlockSpec((nkh, nk, tm), lambda i: (0, 0, i)),
                   pl.BlockSpec((d, tm), lambda i: (0, i))],
        compiler_params=_cparams("parallel"),
        name="peer_scores",
    )(x, g, sh, sc, w_q, keys)


def _remove_max_rounds(s, k, idx, want_round):
    cur = s
    rnd = jnp.full(s.shape, float(k), F32) if want_round else None
    vals = []
    for a in range(k):
        m = jnp.max(cur, axis=0, keepdims=True)
        if idx is None:
            hit = cur == m
        else:
            first = jnp.min(jnp.where(cur == m, idx, float(s.shape[0])), axis=0, keepdims=True)
            hit = idx == first
        if want_round:
            rnd = jnp.where(hit, float(a), rnd)
        cur = jnp.where(hit, NEG_INF, cur)
        vals.append(m)
    return vals, rnd, cur


def _rows_from_list(vals, lo, n, ridx):
    out = jnp.zeros(ridx.shape, F32)
    for r in range(n):
        out = jnp.where(ridx == float(r), vals[lo + r], out)
    return out


def _peer_select_one(s1, s2, topk, idx, cidx, valid, ridx_k, ridx_s):
    sub = V7X_SUBLANES
    stair = [topk // (a + 1) for a in range(topk)]
    v1, rank1, cur1 = _remove_max_rounds(s1, topk, idx, idx is not None)
    v2, rank2, cur2 = _remove_max_rounds(s2, topk, idx, True)
    v2all = _rows_from_list(v2, 0, topk, ridx_k)
    v2lo = _rows_from_list(v2, 0, sub, ridx_s)
    v1hi = _rows_from_list(v1, sub, topk - sub, ridx_s)
    blocks = [v1[0] + v2all] + [v1[a] + v2lo for a in range(1, sub)] + [v1hi + v2[0]]
    cand = jnp.where(valid, jnp.concatenate(blocks, axis=0), NEG_INF)
    tops, _, curc = _remove_max_rounds(cand, topk, cidx, False)
    taken = jnp.where(jnp.logical_and(curc == NEG_INF, valid), 1.0, 0.0)
    z = jnp.ones_like(tops[0])
    for m in tops[1:]:
        z = z + jnp.exp(m - tops[0])
    n1 = jnp.zeros(s1.shape, F32)
    total = jnp.zeros_like(z)
    for a in range(topk):
        if a == 0:
            n_a = jnp.sum(taken[0:topk, :], axis=0, keepdims=True)
        elif a < sub:
            base = topk + sub * (a - 1)
            n_a = jnp.sum(taken[base:base + sub, :], axis=0, keepdims=True)
        else:
            row = topk + sub * (sub - 1) + (a - sub)
            n_a = taken[row:row + 1, :]
        total = total + n_a
        n1 = jnp.where((s1 == v1[a]) if idx is None else (rank1 == float(a)), n_a, n1)
    if idx is None:
        k = float(topk)
        c1 = jnp.sum(jnp.where(cur1 == NEG_INF, 1.0, 0.0), axis=0, keepdims=True)
        c2 = jnp.sum(jnp.where(cur2 == NEG_INF, 1.0, 0.0), axis=0, keepdims=True)
        bad = jnp.where(jnp.logical_and(jnp.logical_and(c1 == k, c2 == k), total == k), 0.0, 1.0)
    else:
        bad = None
    e2 = jnp.exp(s2 - v2[0])
    e1 = jnp.exp(s1 - v1[0]) / z
    return rank2, e2, n1, e1, bad


def _peer_select_kernel(s_ref, r2_ref, e2_ref, n1_ref, e1_ref, *, heads, topk):
    sub = V7X_SUBLANES
    nk = s_ref.shape[1]
    lanes = s_ref.shape[2]
    ncand = topk + sub * sub
    idx = lax.broadcasted_iota(jnp.int32, (nk, lanes), 0).astype(F32)
    crow = lax.broadcasted_iota(jnp.int32, (ncand, lanes), 0)
    cidx = crow.astype(F32)
    ridx_k = lax.broadcasted_iota(jnp.int32, (topk, lanes), 0).astype(F32)
    ridx_s = lax.broadcasted_iota(jnp.int32, (sub, lanes), 0).astype(F32)
    valid = crow < topk + sub
    for a in range(2, sub):
        base = topk + sub * (a - 1)
        valid = jnp.logical_or(valid, jnp.logical_and(crow >= base, crow < base + topk // (a + 1)))
    valid = jnp.logical_or(valid, crow >= topk + sub * (sub - 1))

    def write(h, res):
        rank2, e2, n1, e1 = res
        r2_ref[h] = rank2.astype(r2_ref.dtype)
        e2_ref[h] = e2.astype(e2_ref.dtype)
        n1_ref[h] = n1
        e1_ref[h] = e1

    def head_pair(hp, carry):
        hs = (2 * hp, 2 * hp + 1)
        sc = [(s_ref[2 * h], s_ref[2 * h + 1]) for h in hs]
        bad = None
        for h, (s1, s2) in zip(hs, sc):
            *res, b = _peer_select_one(s1, s2, topk, None, None, valid, ridx_k, ridx_s)
            write(h, res)
            bad = b if bad is None else jnp.maximum(bad, b)

        @pl.when(jnp.max(bad) > 0.0)
        def _():
            for h, (s1, s2) in zip(hs, sc):
                *res, _ = _peer_select_one(s1, s2, topk, idx, cidx, valid, ridx_k, ridx_s)
                write(h, res)

        return carry

    lax.fori_loop(0, heads // 2, head_pair, 0)


def _peer_select(scores, heads, topk):
    nkh, nk, t = scores.shape
    tl = _row_tile(t, V7X_LANES)
    shp = jax.ShapeDtypeStruct((heads, nk, t), F32)
    shp_i2 = jax.ShapeDtypeStruct((heads, nk, t), BF16)
    ospec = pl.BlockSpec((heads, nk, tl), lambda i: (0, 0, i))
    return pl.pallas_call(
        functools.partial(_peer_select_kernel, heads=heads, topk=topk),
        out_shape=[shp_i2, shp_i2, shp, shp],
        grid=(t // tl,),
        in_specs=[pl.BlockSpec((nkh, nk, tl), lambda i: (0, 0, i))],
        out_specs=[ospec, ospec, ospec, ospec],
        compiler_params=_cparams("parallel"),
        name="peer_select",
    )(scores)


def _peer_dense_kernel(ht_ref, u_ref, vt_ref, r2_ref, e2_ref, n1_ref, e1_ref, x_ref, gate_ref, o_ref,
                       at0_scr, at1_scr, wt_scr, acc_scr, *, heads, nk, i1_per_step, nblk):
    j = pl.program_id(1)
    at_scr = (at0_scr, at1_scr)

    def activations(dst):
        dst[...] = _dot(u_ref[...], ht_ref[...])

    pack = 2 * V7X_SUBLANES
    tn = wt_scr.shape[1]

    def packed_row(ref, h, r):
        return jnp.broadcast_to(ref[h, r:r + 1, :], (pack, tn)).astype(BF16)[None]

    def mix(src):
        for r in range(i1_per_step):
            rows = pl.ds(r * nk, nk)
            g = None
            for h in range(heads):
                n = packed_row(n1_ref, h, r)
                e1 = packed_row(e1_ref, h, r)
                r2 = r2_ref[h].reshape(nk // pack, pack, tn)
                e2 = e2_ref[h].reshape(nk // pack, pack, tn)
                term = jnp.where(r2 < n, e2, jnp.zeros_like(e2)) * e1
                g = term if g is None else g + term
            g = g.reshape(nk, tn).astype(F32)
            wt_scr[rows, :] = (_gelu_tanh(src[rows, :]) * g).astype(BF16)
        acc_scr[...] += _dot(vt_ref[0], wt_scr[...])

    @pl.when(j == 0)
    def _():
        acc_scr[...] = jnp.zeros_like(acc_scr)
        activations(at_scr[0])

    for parity in range(2):
        @pl.when(jnp.logical_and(jnp.logical_and(j > 0, j < nblk), j % 2 == parity))
        def _():
            activations(at_scr[parity])
            mix(at_scr[1 - parity])

    @pl.when(j == nblk)
    def _():
        mix(at_scr[(nblk - 1) % 2])
        o_ref[...] = x_ref[...] + gate_ref[0] * jnp.transpose(acc_scr[...])


def _peer_dense(ht, u, vt, r2, e2, n1, e1, x, gate, seq):
    t, d = x.shape
    heads, nk, _ = r2.shape
    i1_per_step = V7X_SUBLANES
    nblk, _, te = vt.shape
    assert te == i1_per_step * nk and u.shape == (nblk * te, d)
    tn = _row_tile(min(t, seq) if seq > 1 else t, 512)
    clamp = lambda b: jnp.clip(b, 0, nblk - 1)
    tok = pl.BlockSpec((heads, nk, tn), lambda i, j: (0, 0, i))
    per_i1 = pl.BlockSpec((heads, i1_per_step, tn), lambda i, j: (0, clamp(j - 1), i))
    return pl.pallas_call(
        functools.partial(_peer_dense_kernel, heads=heads, nk=nk, i1_per_step=i1_per_step, nblk=nblk),
        out_shape=jax.ShapeDtypeStruct((t, d), F32),
        grid=(t // tn, nblk + 1),
        in_specs=[
            pl.BlockSpec((d, tn), lambda i, j: (0, i)),
            pl.BlockSpec((te, d), lambda i, j: (clamp(j), 0)),
            pl.BlockSpec((1, d, te), lambda i, j: (clamp(j - 1), 0, 0)),
            tok, tok, per_i1, per_i1,
            pl.BlockSpec((tn, d), lambda i, j: (i, 0)),
            _mod_specs(tn, d, seq),
        ],
        out_specs=pl.BlockSpec((tn, d), lambda i, j: (i, 0)),
        scratch_shapes=[pltpu.VMEM((te, tn), F32), pltpu.VMEM((te, tn), F32),
                        pltpu.VMEM((te, tn), BF16), pltpu.VMEM((d, tn), F32)],
        compiler_params=_cparams("parallel", "arbitrary"),
        name="peer_dense",
    )(ht, u, vt, r2, e2, n1, e1, x, gate)


def _rmsnorm_kernel(x_ref, g_ref, o_ref):
    x = x_ref[...]
    o_ref[...] = (x * lax.rsqrt(jnp.mean(x * x, axis=-1, keepdims=True) + EPS)) * g_ref[...]


def _rmsnorm(x, g):
    t, d = x.shape
    tm = _row_tile(t, 1024)
    return pl.pallas_call(
        _rmsnorm_kernel,
        out_shape=jax.ShapeDtypeStruct((t, d), F32),
        grid=(t // tm,),
        in_specs=[pl.BlockSpec((tm, d), lambda i: (i, 0)), pl.BlockSpec((1, d), lambda i: (0, 0))],
        out_specs=pl.BlockSpec((tm, d), lambda i: (i, 0)),
        compiler_params=_cparams("parallel"),
        name="final_rmsnorm",
    )(x, g)


def _mods(mod, lo, hi, seq):
    d = mod.shape[1] // 6
    rows = mod[lo:hi]
    parts = [rows[:, k * d:(k + 1) * d] for k in range(6)]
    if seq == 1:
        return [p[None, :, :] for p in parts]
    return [p[:, None, :] for p in parts]


def kernel(x_prompt, x_sample, c_prompt, c_sample, state_ret, state_ssm_re, state_ssm_im, norm_g, final_g, w_ada, b_ada, ret_w_in, ret_w_out, ssm_w_in, ssm_lam_re, ssm_lam_im, ssm_log_dt, ssm_b_re, ssm_b_im, ssm_c_re, ssm_c_im, ssm_d, ssm_w_glu, peer_w_q, peer_key1, peer_key2, peer_u, peer_v):
    batch, seq, d = x_prompt.shape
    dbatch, dseq, _ = x_sample.shape
    assert dseq == 1 and seq % RET_CHUNK == 0
    depth = w_ada.shape[0]
    heads = peer_key1.shape[1]

    c_all = jnp.concatenate([c_prompt, c_sample], axis=0)
    pad = (-c_all.shape[0]) % V7X_SUBLANES
    c_all = jnp.pad(c_all, ((0, pad), (0, 0)))
    mod = _ada(c_all, w_ada.astype(BF16), b_ada)

    groups = [
        dict(x=x_prompt.reshape(batch * seq, d), lo=0, hi=batch, seq=seq, nb=batch),
        dict(x=x_sample.reshape(dbatch, d), lo=batch, hi=batch + dbatch, seq=1, nb=dbatch),
    ]
    outs = [dict(ret=[], re=[], im=[]) for _ in groups]
    ret_sample = None

    for i in range(depth):
        jm = i // N_MIXERS
        is_ret = i % N_MIXERS == 0
        w_q = peer_w_q[i].astype(BF16)
        keys = jnp.stack([peer_key1[i], peer_key2[i]], axis=1).reshape(2 * heads, PEER_NKEYS, -1).astype(BF16)
        u_tab = peer_u[i].astype(BF16)
        te = V7X_SUBLANES * PEER_NKEYS
        vt_tab = jnp.transpose(peer_v[i].reshape(-1, te, d), (0, 2, 1)).astype(BF16)
        g_mix = norm_g[i, 0].reshape(1, d)
        g_peer = norm_g[i, 1].reshape(1, d)
        if is_ret:
            w_in = ret_w_in[jm].astype(BF16)
            w_out = ret_w_out[jm].astype(BF16)
        else:
            w_in = ssm_w_in[jm].astype(BF16)
            w_out = ssm_w_glu[jm].astype(BF16)
            a_re, a_im, bbr, bbi = _ssm_discretize(ssm_lam_re[jm], ssm_lam_im[jm], ssm_log_dt[jm],
                                                   ssm_b_re[jm], ssm_b_im[jm])
            per = V7X_MXU_DIM // SSM_GROUP
            disc = (_block_diag(bbr, per).astype(BF16), _block_diag(bbi, per).astype(BF16),
                    _block_diag(jnp.transpose(ssm_c_re[jm], (0, 2, 1)), per).astype(BF16),
                    _block_diag(jnp.transpose(ssm_c_im[jm], (0, 2, 1)), per).astype(BF16),
                    ssm_d[jm].reshape(1, d), a_re.reshape(1, -1), a_im.reshape(1, -1))

        for gi, grp in enumerate(groups):
            x = grp["x"]
            sq = grp["seq"]
            sh1, sc1, g1, sh2, sc2, g2 = _mods(mod[i], grp["lo"], grp["hi"], sq)
            tmaj = sq > 1 and not is_ret
            proj = _nm_matmul(x, g_mix, sh1, sc1, w_in, sq, time_major=tmaj)
            if is_ret:
                if sq > 1:
                    y, s = _retention_prompt(proj, grp["nb"], sq, d)
                    outs[gi]["ret"].append(s)
                else:
                    y, ret_sample = _retention_step(proj, state_ret, jm, ret_sample, d)
                x = _matmul_res(y, w_out, x, g1, sq, glu=False)
            else:
                if sq > 1:
                    z, fr, fi = _ssm_prompt(proj, disc, grp["nb"], sq)
                else:
                    z, fr, fi = _ssm_step(proj, state_ssm_re[jm].reshape(dbatch, -1),
                                          state_ssm_im[jm].reshape(dbatch, -1), disc)
                outs[gi]["re"].append(fr.reshape(grp["nb"], -1, SSM_STATE))
                outs[gi]["im"].append(fi.reshape(grp["nb"], -1, SSM_STATE))
                x = _matmul_res(z, w_out, x, g1, sq, glu=True, a_time_major=tmaj)
            scores, h2 = _peer_scores(x, g_peer, sh2, sc2, w_q, keys, sq)
            r2, e2, n1, e1 = _peer_select(scores, heads, PEER_TOPK)
            x = _peer_dense(h2, u_tab, vt_tab, r2, e2, n1, e1, x, g2, sq)
            grp["x"] = x

    fg = final_g.reshape(1, d)
    y_prompt = _rmsnorm(groups[0]["x"], fg).reshape(batch, seq, d)
    y_sample = _rmsnorm(groups[1]["x"], fg).reshape(dbatch, 1, d)
    return (y_prompt, y_sample,
            jnp.stack(outs[0]["ret"]), ret_sample,
            jnp.stack(outs[0]["re"]), jnp.stack(outs[0]["im"]),
            jnp.stack(outs[1]["re"]), jnp.stack(outs[1]["im"]))
```

```python
import functools
import math

import jax
import jax.numpy as jnp
from jax import lax
from jax.experimental import pallas as pl
from jax.experimental.pallas import tpu as pltpu

F32 = jnp.float32
BF16 = jnp.bfloat16

EPS = 1e-6
ROPE_BASE = 10000.0
PAST_LEN = 16384
RET_HEADS = 4
RET_CHUNK = 128
SSM_GROUP = 16
SSM_STATE = 64
PEER_HEADS = 8
PEER_NKEYS = 128
PEER_TOPK = 16
N_MIXERS = 2

V7X_LANES = 128
V7X_SUBLANES = 8
V7X_MXU_DIM = 256
V7X_VMEM_LIMIT_BYTES = 56 * 1024 * 1024

NEG_INF = float("-inf")


def _cparams(*sem):
    return pltpu.CompilerParams(dimension_semantics=sem, vmem_limit_bytes=V7X_VMEM_LIMIT_BYTES)


def _dot(a, b):
    return jnp.dot(a, b, preferred_element_type=F32)


def _dot_nt(a, b):
    return lax.dot_general(a, b, (((1,), (1,)), ((), ())), preferred_element_type=F32)


def _dot_tn(a, b):
    return lax.dot_general(a, b, (((0,), (0,)), ((), ())), preferred_element_type=F32)


def _sigmoid(x):
    return 1.0 / (1.0 + jnp.exp(-x))


def _gelu_tanh(x):
    c = math.sqrt(2.0 / math.pi)
    hx = 0.5 * x
    return hx + hx * jnp.tanh(x * (c + (c * 0.044715) * (x * x)))


def _row_tile(t, pref):
    if t <= pref:
        return t
    tile = pref
    while t % tile:
        tile //= 2
    return tile


def _ada_kernel(c_ref, w_ref, b_ref, o_ref):
    c = c_ref[...]
    sc = (c * _sigmoid(c)).astype(BF16)
    o_ref[0] = _dot(sc, w_ref[0].astype(BF16)) + b_ref[0]


def _ada(c_all, w_ada, b_ada):
    m, d = c_all.shape
    depth, _, n = w_ada.shape
    tn = 1024
    return pl.pallas_call(
        _ada_kernel,
        out_shape=jax.ShapeDtypeStruct((depth, m, n), F32),
        grid=(depth, n // tn),
        in_specs=[
            pl.BlockSpec((m, d), lambda l, j: (0, 0)),
            pl.BlockSpec((1, d, tn), lambda l, j: (l, 0, j)),
            pl.BlockSpec((1, 1, tn), lambda l, j: (l, 0, j)),
        ],
        out_specs=pl.BlockSpec((1, m, tn), lambda l, j: (l, 0, j)),
        compiler_params=_cparams("parallel", "parallel"),
        name="ada_mod",
    )(c_all, w_ada, b_ada.reshape(depth, 1, n))


def _norm_mod(x, g, sh, sc):
    y = x * lax.rsqrt(jnp.mean(x * x, axis=-1, keepdims=True) + EPS)
    return (y * g) * (1.0 + sc) + sh


def _nm_matmul_kernel(x_ref, g_ref, sh_ref, sc_ref, w_ref, o_ref, h_scr):
    @pl.when(pl.program_id(1) == 0)
    def _():
        h_scr[...] = _norm_mod(x_ref[...], g_ref[...], sh_ref[0], sc_ref[0]).astype(BF16)

    o_ref[...] = _dot(h_scr[...], w_ref[...])


def _mod_specs(tm, d, seq):
    if seq == 1:
        return pl.BlockSpec((1, tm, d), lambda i, *_: (0, i, 0))
    per = seq // tm
    return pl.BlockSpec((1, 1, d), lambda i, *_: (i // per, 0, 0))


def _nm_matmul(x, g, sh, sc, w, seq, time_major=False):
    t, d = x.shape
    n = w.shape[1]
    tm = _row_tile(min(t, seq) if seq > 1 else t, 512)
    tn = min(n, 1024)
    if time_major:
        per, nj = seq // tm, n // tn
        out_shape = jax.ShapeDtypeStruct((seq, (t // seq) * n), F32)
        out_spec = pl.BlockSpec((tm, tn), lambda i, j: (i % per, (i // per) * nj + j))
    else:
        out_shape = jax.ShapeDtypeStruct((t, n), F32)
        out_spec = pl.BlockSpec((tm, tn), lambda i, j: (i, j))
    res = pl.pallas_call(
        _nm_matmul_kernel,
        out_shape=out_shape,
        grid=(t // tm, n // tn),
        in_specs=[
            pl.BlockSpec((tm, d), lambda i, j: (i, 0)),
            pl.BlockSpec((1, d), lambda i, j: (0, 0)),
            _mod_specs(tm, d, seq),
            _mod_specs(tm, d, seq),
            pl.BlockSpec((d, tn), lambda i, j: (0, j)),
        ],
        out_specs=out_spec,
        scratch_shapes=[pltpu.VMEM((tm, d), BF16)],
        compiler_params=_cparams("parallel", "arbitrary"),
        name="norm_mod_matmul",
    )(x, g, sh, sc, w)
    return res.reshape(t, n) if time_major else res


def _matmul_res_kernel(a_ref, w_ref, x_ref, gate_ref, o_ref, *, glu):
    r = _dot(a_ref[...], w_ref[...])
    if glu:
        half = r.shape[1] // 2
        r = r[:, :half] * _sigmoid(r[:, half:])
    o_ref[...] = x_ref[...] + gate_ref[0] * r


def _matmul_res(a, w, x, gate, seq, glu, a_time_major=False, tm_pref=512):
    t, k = a.shape
    n = w.shape[1]
    d = x.shape[1]
    tm = _row_tile(min(t, seq) if seq > 1 else t, tm_pref)
    if a_time_major:
        per = seq // tm
        a = a.reshape(seq, (t // seq) * k)
        a_spec = pl.BlockSpec((tm, k), lambda i: (i % per, i // per))
    else:
        a_spec = pl.BlockSpec((tm, k), lambda i: (i, 0))
    return pl.pallas_call(
        functools.partial(_matmul_res_kernel, glu=glu),
        out_shape=jax.ShapeDtypeStruct((t, d), F32),
        grid=(t // tm,),
        in_specs=[
            a_spec,
            pl.BlockSpec((k, n), lambda i: (0, 0)),
            pl.BlockSpec((tm, d), lambda i: (i, 0)),
            _mod_specs(tm, d, seq),
        ],
        out_specs=pl.BlockSpec((tm, d), lambda i: (i, 0)),
        compiler_params=_cparams("parallel"),
        name="matmul_residual",
    )(a, w, x, gate)


def _rotary(x, cos, sin):
    half = x.shape[-1] // 2
    x1 = x[:, :half]
    x2 = x[:, half:]
    return jnp.concatenate([x1 * cos - x2 * sin, x2 * cos + x1 * sin], axis=-1)


def _group_norm_gate(o, g):
    mu = jnp.mean(o, axis=-1, keepdims=True)
    var = jnp.mean(jnp.square(o - mu), axis=-1, keepdims=True)
    on = (o - mu) * lax.rsqrt(var + EPS)
    return (g * _sigmoid(g)) * on


def _ret_chunk_kernel(q_ref, k_ref, v_ref, g_ref, cos_ref, sin_ref, dmask_ref, cross_ref, tail_ref,
                      cdec_ref, o_ref, sfin_ref, s_scr, *, n_sub, chunk, k_scale):
    c = pl.program_id(2)

    @pl.when(c == 0)
    def _():
        s_scr[...] = jnp.zeros_like(s_scr)

    dmask = dmask_ref[0]
    cross = cross_ref[0]
    tail = tail_ref[0]
    cdec = cdec_ref[0]
    for j in range(n_sub):
        rows = pl.ds(j * chunk, chunk)
        cos = cos_ref[rows, :]
        sin = sin_ref[rows, :]
        qc = _rotary(q_ref[rows, :], cos, sin)
        kc = _rotary(k_ref[rows, :], cos, sin) * k_scale
        vb = v_ref[rows, :].astype(BF16)
        qb = qc.astype(BF16)
        att = _dot_nt(qb, kc.astype(BF16)) * dmask
        s = s_scr[...]
        o = _dot(att.astype(BF16), vb) + _dot(qb, s.astype(BF16)) * cross
        s_scr[...] = s * cdec + _dot_tn((kc * tail).astype(BF16), vb)
        o_ref[rows, :] = _group_norm_gate(o, g_ref[rows, :]).astype(BF16)

    @pl.when(c == pl.num_programs(2) - 1)
    def _():
        sfin_ref[0, 0] = s_scr[...]


def _ret_tables(heads, chunk):
    lg = jnp.log1p(-jnp.exp2(-5.0 - jnp.arange(heads, dtype=F32)))
    i = jnp.arange(chunk, dtype=F32)
    diff = i[:, None] - i[None, :]
    dmask = jnp.where(diff >= 0, jnp.exp(lg[:, None, None] * jnp.maximum(diff, 0.0)), 0.0)
    cross = jnp.exp(lg[:, None] * (i[None, :] + 1.0))[:, :, None]
    tail = jnp.exp(lg[:, None] * (chunk - 1.0 - i[None, :]))[:, :, None]
    cdec = jnp.exp(lg * chunk)
    return dmask, cross, tail, cdec


def _rope_tables(pos, half):
    inv = ROPE_BASE ** (-jnp.arange(half, dtype=F32) / half)
    ang = pos.astype(F32)[:, None] * inv[None, :]
    return jnp.cos(ang), jnp.sin(ang)


def _retention_prompt(proj, batch, seq, d):
    heads = RET_HEADS
    dk = d // heads
    dv = 2 * dk
    chunk = RET_CHUNK
    rows = _row_tile(seq, 4 * chunk)
    n_sub = rows // chunk
    nblk = seq // rows
    cos, sin = _rope_tables(jnp.arange(seq, dtype=jnp.int32), dk // 2)
    dmask, cross, tail, cdec = _ret_tables(heads, chunk)
    cdec = jnp.broadcast_to(cdec[:, None, None], (heads, 1, dv))
    kq = heads
    vq = (2 * heads * dk) // dv
    gq = vq + heads
    o, sfin = pl.pallas_call(
        functools.partial(_ret_chunk_kernel, n_sub=n_sub, chunk=chunk, k_scale=dk ** -0.5),
        out_shape=[jax.ShapeDtypeStruct((batch * seq, heads * dv), BF16),
                   jax.ShapeDtypeStruct((batch, heads, dk, dv), F32)],
        grid=(batch, heads, nblk),
        in_specs=[
            pl.BlockSpec((rows, dk), lambda b, h, c: (b * nblk + c, h)),
            pl.BlockSpec((rows, dk), lambda b, h, c: (b * nblk + c, kq + h)),
            pl.BlockSpec((rows, dv), lambda b, h, c: (b * nblk + c, vq + h)),
            pl.BlockSpec((rows, dv), lambda b, h, c: (b * nblk + c, gq + h)),
            pl.BlockSpec((rows, dk // 2), lambda b, h, c: (c, 0)),
            pl.BlockSpec((rows, dk // 2), lambda b, h, c: (c, 0)),
            pl.BlockSpec((1, chunk, chunk), lambda b, h, c: (h, 0, 0)),
            pl.BlockSpec((1, chunk, 1), lambda b, h, c: (h, 0, 0)),
            pl.BlockSpec((1, chunk, 1), lambda b, h, c: (h, 0, 0)),
            pl.BlockSpec((1, 1, dv), lambda b, h, c: (h, 0, 0)),
        ],
        out_specs=[
            pl.BlockSpec((rows, dv), lambda b, h, c: (b * nblk + c, h)),
            pl.BlockSpec((1, 1, dk, dv), lambda b, h, c: (b, h, 0, 0)),
        ],
        scratch_shapes=[pltpu.VMEM((dk, dv), F32)],
        compiler_params=_cparams("parallel", "parallel", "arbitrary"),
        name="retention_chunks",
    )(proj, proj, proj, proj, cos, sin, dmask, cross, tail, cdec)
    return o, sfin


def _ret_step_kernel(q_ref, k_ref, v_ref, g_ref, cos_ref, sin_ref, gam_ref, s0_ref, *rest, nb, k_scale):
    o_ref, s_ref = rest[-2:]
    cos = cos_ref[...]
    sin = sin_ref[...]
    gam = gam_ref[0]
    qc = _rotary(q_ref[...], cos, sin)
    kc = _rotary(k_ref[...], cos, sin) * k_scale
    qr = qc.astype(BF16).astype(F32)
    kr = kc.astype(BF16).astype(F32)
    vr = v_ref[...].astype(BF16).astype(F32)
    att = jnp.sum(qr * kr, axis=-1, keepdims=True)
    att = att.astype(BF16).astype(F32)
    dk = qc.shape[1]
    eye = (lax.broadcasted_iota(jnp.int32, (dk, dk), 0) == lax.broadcasted_iota(jnp.int32, (dk, dk), 1))
    rows8 = lax.broadcasted_iota(jnp.int32, (V7X_SUBLANES, dk), 0)
    o = jnp.zeros(vr.shape, F32)
    brow = lax.broadcasted_iota(jnp.int32, vr.shape, 0)
    for b in range(nb):
        s0 = s0_ref[0, b, 0]
        qrow = qc[b:b + 1, :]
        q8 = jnp.where(rows8 == 0, qrow, 0.0).astype(BF16)
        qs = _dot(q8, s0.astype(BF16))[0:1, :]
        o = jnp.where(brow == b, att[b:b + 1, :] * vr[b:b + 1, :] + qs * gam, o)
        kcol = jnp.sum(jnp.where(eye, kr[b:b + 1, :], 0.0), axis=1, keepdims=True)
        s_ref[0, b, 0] = s0 * gam + kcol * vr[b:b + 1, :]
    o_ref[...] = _group_norm_gate(o, g_ref[...]).astype(BF16)


def _retention_step(proj, states, layer, new_states, d):
    t = proj.shape[0]
    heads = RET_HEADS
    dk = d // heads
    dv = 2 * dk
    nb = V7X_SUBLANES
    cos, sin = _rope_tables(PAST_LEN + jnp.arange(1, dtype=jnp.int32), dk // 2)
    lg = jnp.log1p(-jnp.exp2(-5.0 - jnp.arange(heads, dtype=F32)))
    gam = jnp.broadcast_to(jnp.exp(lg)[:, None, None], (heads, 1, dv))
    kq = heads
    vq = (2 * heads * dk) // dv
    gq = vq + heads
    state_spec = pl.BlockSpec((1, nb, 1, dk, dv), lambda i, h: (layer, i, h, 0, 0))
    in_specs = [
        pl.BlockSpec((nb, dk), lambda i, h: (i, h)),
        pl.BlockSpec((nb, dk), lambda i, h: (i, kq + h)),
        pl.BlockSpec((nb, dv), lambda i, h: (i, vq + h)),
        pl.BlockSpec((nb, dv), lambda i, h: (i, gq + h)),
        pl.BlockSpec((1, dk // 2), lambda i, h: (0, 0)),
        pl.BlockSpec((1, dk // 2), lambda i, h: (0, 0)),
        pl.BlockSpec((1, 1, dv), lambda i, h: (h, 0, 0)),
        state_spec,
    ]
    args = [proj, proj, proj, proj, cos, sin, gam, states]
    aliases = {}
    if new_states is not None:
        in_specs.append(pl.BlockSpec(memory_space=pl.ANY))
        args.append(new_states)
        aliases = {len(args) - 1: 1}
    o, s = pl.pallas_call(
        functools.partial(_ret_step_kernel, nb=nb, k_scale=dk ** -0.5),
        out_shape=[jax.ShapeDtypeStruct((t, heads * dv), BF16),
                   jax.ShapeDtypeStruct(states.shape, F32)],
        grid=(t // nb, heads),
        in_specs=in_specs,
        out_specs=[pl.BlockSpec((nb, dv), lambda i, h: (i, h)), state_spec],
        input_output_aliases=aliases,
        compiler_params=_cparams("parallel", "parallel"),
        name="retention_step",
    )(*args)
    return o, s


def _ssm_disc_kernel(lr_ref, li_ref, ldt_ref, br_ref, bi_ref, are_ref, aim_ref, bbr_ref, bbi_ref):
    lr = lr_ref[...]
    li = li_ref[...]
    dt = jnp.exp(ldt_ref[...])
    mag = jnp.exp(lr * dt)
    a_re = mag * jnp.cos(li * dt)
    a_im = mag * jnp.sin(li * dt)
    den = lr * lr + li * li
    nr = a_re - 1.0
    c_re = (nr * lr + a_im * li) / den
    c_im = (a_im * lr - nr * li) / den
    are_ref[...] = a_re
    aim_ref[...] = a_im
    br = br_ref[...]
    bi = bi_ref[...]
    bbr_ref[...] = c_re * br - c_im * bi
    bbi_ref[...] = c_re * bi + c_im * br


def _ssm_discretize(lam_re, lam_im, log_dt, b_re, b_im):
    g, n = lam_re.shape
    p = b_re.shape[2]
    brt = jnp.transpose(b_re, (0, 2, 1))
    bit = jnp.transpose(b_im, (0, 2, 1))
    a_re, a_im, bbr, bbi = pl.pallas_call(
        _ssm_disc_kernel,
        out_shape=[jax.ShapeDtypeStruct((g, 1, n), F32), jax.ShapeDtypeStruct((g, 1, n), F32),
                   jax.ShapeDtypeStruct((g, p, n), F32), jax.ShapeDtypeStruct((g, p, n), F32)],
        name="ssm_discretize",
    )(lam_re.reshape(g, 1, n), lam_im.reshape(g, 1, n), log_dt.reshape(g, 1, 1), brt, bit)
    return a_re.reshape(g, n), a_im.reshape(g, n), bbr, bbi


def _block_diag(w, per):
    g, a, b = w.shape
    w = w.reshape(g // per, per, a, b)
    eye = jnp.eye(per, dtype=w.dtype)
    return jnp.einsum('cgab,gh->cgahb', w, eye).reshape(g // per, per * a, per * b)


def _ssm_bu(ub, bdr_ref, bdi_ref, nblk, kw):
    bur = jnp.concatenate([_dot(ub[:, c * kw:(c + 1) * kw], bdr_ref[c]) for c in range(nblk)], axis=1)
    bui = jnp.concatenate([_dot(ub[:, c * kw:(c + 1) * kw], bdi_ref[c]) for c in range(nblk)], axis=1)
    return bur, bui


def _ssm_y(hr, hi, cdr_ref, cdi_ref, nblk, kw):
    hrb = hr.astype(BF16)
    hib = hi.astype(BF16)
    return jnp.concatenate(
        [_dot(hrb[:, c * kw:(c + 1) * kw], cdr_ref[c]) - _dot(hib[:, c * kw:(c + 1) * kw], cdi_ref[c])
         for c in range(nblk)], axis=1)


def _ssm_scan_kernel(u_ref, bdr_ref, bdi_ref, cdr_ref, cdi_ref, d_ref, ar_ref, ai_ref,
                     z_ref, fr_ref, fi_ref, hr_scr, hi_scr, cr_scr, ci_scr, *, nblk, nb, lane_chunk):
    step = pl.program_id(0)

    @pl.when(step == 0)
    def _():
        cr_scr[...] = jnp.zeros_like(cr_scr)
        ci_scr[...] = jnp.zeros_like(ci_scr)

    u = u_ref[...]
    kw_in = u.shape[1] // nblk
    bur, bui = _ssm_bu(u.astype(BF16), bdr_ref, bdi_ref, nblk, kw_in)
    hr_scr[...] = bur
    hi_scr[...] = bui

    rows_total, gn = hr_scr.shape
    for c in range(gn // lane_chunk):
        lanes = pl.ds(c * lane_chunk, lane_chunk)
        ar = jnp.broadcast_to(ar_ref[:, lanes], (nb, lane_chunk))
        ai = jnp.broadcast_to(ai_ref[:, lanes], (nb, lane_chunk))

        def time_step(s, carry):
            hr, hi = carry
            rows = pl.ds(pl.multiple_of(s * nb, nb), nb)
            nr = hr_scr[rows, lanes] + (ar * hr - ai * hi)
            ni = hi_scr[rows, lanes] + (ar * hi + ai * hr)
            hr_scr[rows, lanes] = nr
            hi_scr[rows, lanes] = ni
            return nr, ni

        hr, hi = lax.fori_loop(0, rows_total // nb, time_step, (cr_scr[:, lanes], ci_scr[:, lanes]))
        cr_scr[:, lanes] = hr
        ci_scr[:, lanes] = hi

    kw_out = gn // nblk
    y = _ssm_y(hr_scr[...], hi_scr[...], cdr_ref, cdi_ref, nblk, kw_out) + d_ref[...] * u
    z_ref[...] = _gelu_tanh(y).astype(BF16)

    @pl.when(step == pl.num_programs(0) - 1)
    def _():
        fr_ref[...] = cr_scr[...]
        fi_ref[...] = ci_scr[...]


def _ssm_prompt(u_tm, disc, batch, seq):
    t, d = u_tm.shape
    bdr, bdi, cdr, cdi, dvec, a_re, a_im = disc
    nblk = bdr.shape[0]
    gn = a_re.shape[1]
    assert batch == V7X_SUBLANES, "one time step of all sequences must fill the sublanes of a vreg"
    rows = batch * _row_tile(seq, 32)
    full = lambda a: pl.BlockSpec(a.shape, lambda s: (0,) * a.ndim)
    z, fr, fi = pl.pallas_call(
        functools.partial(_ssm_scan_kernel, nblk=nblk, nb=batch, lane_chunk=4 * V7X_LANES),
        out_shape=[jax.ShapeDtypeStruct((t, d), BF16),
                   jax.ShapeDtypeStruct((batch, gn), F32),
                   jax.ShapeDtypeStruct((batch, gn), F32)],
        grid=(t // rows,),
        in_specs=[pl.BlockSpec((rows, d), lambda s: (s, 0)),
                  full(bdr), full(bdi), full(cdr), full(cdi), full(dvec), full(a_re), full(a_im)],
        out_specs=[pl.BlockSpec((rows, d), lambda s: (s, 0)),
                   pl.BlockSpec((batch, gn), lambda s: (0, 0)),
                   pl.BlockSpec((batch, gn), lambda s: (0, 0))],
        scratch_shapes=[pltpu.VMEM((rows, gn), F32), pltpu.VMEM((rows, gn), F32),
                        pltpu.VMEM((batch, gn), F32), pltpu.VMEM((batch, gn), F32)],
        compiler_params=_cparams("arbitrary"),
        name="ssm_scan",
    )(u_tm, bdr, bdi, cdr, cdi, dvec, a_re, a_im)
    return z, fr, fi


def _ssm_step_kernel(u_ref, h0r_ref, h0i_ref, bdr_ref, bdi_ref, cdr_ref, cdi_ref, d_ref, ar_ref, ai_ref,
                     z_ref, fr_ref, fi_ref, *, nblk):
    u = u_ref[...]
    kw_in = u.shape[1] // nblk
    bur, bui = _ssm_bu(u.astype(BF16), bdr_ref, bdi_ref, nblk, kw_in)
    ar = ar_ref[...]
    ai = ai_ref[...]
    h0r = h0r_ref[...]
    h0i = h0i_ref[...]
    hr = bur + (ar * h0r - ai * h0i)
    hi = bui + (ar * h0i + ai * h0r)
    fr_ref[...] = hr
    fi_ref[...] = hi
    kw_out = hr.shape[1] // nblk
    y = _ssm_y(hr, hi, cdr_ref, cdi_ref, nblk, kw_out) + d_ref[...] * u
    z_ref[...] = _gelu_tanh(y).astype(BF16)


def _ssm_step(u, h0r, h0i, disc):
    t, d = u.shape
    bdr, bdi, cdr, cdi, dvec, a_re, a_im = disc
    gn = a_re.shape[1]
    return pl.pallas_call(
        functools.partial(_ssm_step_kernel, nblk=bdr.shape[0]),
        out_shape=[jax.ShapeDtypeStruct((t, d), BF16),
                   jax.ShapeDtypeStruct((t, gn), F32), jax.ShapeDtypeStruct((t, gn), F32)],
        name="ssm_step",
        compiler_params=pltpu.CompilerParams(vmem_limit_bytes=V7X_VMEM_LIMIT_BYTES),
    )(u, h0r, h0i, bdr, bdi, cdr, cdi, dvec, a_re, a_im)


def _peer_scores_kernel(x_ref, g_ref, sh_ref, sc_ref, w_ref, key_ref, s_ref, ht_ref, *, nkh, dq):
    h = _norm_mod(x_ref[...], g_ref[...], sh_ref[0], sc_ref[0])
    ht_ref[...] = jnp.transpose(h).astype(BF16)
    q = _dot(h.astype(BF16), w_ref[...]).astype(BF16)
    for c in range(nkh):
        s_ref[c] = _dot_nt(key_ref[c], q[:, c * dq:(c + 1) * dq])


def _peer_scores(x, g, sh, sc, w_q, keys, seq):
    t, d = x.shape
    n = w_q.shape[1]
    nkh, nk, dq = keys.shape
    tm = _row_tile(min(t, seq) if seq > 1 else t, 512)
    return pl.pallas_call(
        functools.partial(_peer_scores_kernel, nkh=nkh, dq=dq),
        out_shape=[jax.ShapeDtypeStruct((nkh, nk, t), F32), jax.ShapeDtypeStruct((d, t), BF16)],
        grid=(t // tm,),
        in_specs=[pl.BlockSpec((tm, d), lambda i: (i, 0)),
                  pl.BlockSpec((1, d), lambda i: (0, 0)),
                  _mod_specs(tm, d, seq),
                  _mod_specs(tm, d, seq),
                  pl.BlockSpec((d, n), lambda i: (0, 0)),
                  pl.BlockSpec((nkh, nk, dq), lambda i: (0, 0, 0))],
        out_specs=[pl.BlockSpec((nkh, nk, tm), lambda i: (0, 0, i)),
                   pl.BlockSpec((d, tm), lambda i: (0, i))],
        compiler_params=_cparams("parallel"),
        name="peer_scores",
    )(x, g, sh, sc, w_q, keys)


def _remove_max_rounds(s, k, idx, want_round):
    cur = s
    rnd = jnp.full(s.shape, float(k), F32) if want_round else None
    vals = []
    for a in range(k):
        m = jnp.max(cur, axis=0, keepdims=True)
        if idx is None:
            hit = cur == m
        else:
            first = jnp.min(jnp.where(cur == m, idx, float(s.shape[0])), axis=0, keepdims=True)
            hit = idx == first
        if want_round:
            rnd = jnp.where(hit, float(a), rnd)
        cur = jnp.where(hit, NEG_INF, cur)
        vals.append(m)
    return vals, rnd, cur


def _rows_from_list(vals, lo, n, ridx):
    out = jnp.zeros(ridx.shape, F32)
    for r in range(n):
        out = jnp.where(ridx == float(r), vals[lo + r], out)
    return out


def _peer_select_one(s1, s2, topk, idx, cidx, valid, ridx_k, ridx_s):
    sub = V7X_SUBLANES
    stair = [topk // (a + 1) for a in range(topk)]
    v1, rank1, cur1 = _remove_max_rounds(s1, topk, idx, idx is not None)
    v2, rank2, cur2 = _remove_max_rounds(s2, topk, idx, True)
    v2all = _rows_from_list(v2, 0, topk, ridx_k)
    v2lo = _rows_from_list(v2, 0, sub, ridx_s)
    v1hi = _rows_from_list(v1, sub, topk - sub, ridx_s)
    blocks = [v1[0] + v2all] + [v1[a] + v2lo for a in range(1, sub)] + [v1hi + v2[0]]
    cand = jnp.where(valid, jnp.concatenate(blocks, axis=0), NEG_INF)
    tops, _, curc = _remove_max_rounds(cand, topk, cidx, False)
    taken = jnp.where(jnp.logical_and(curc == NEG_INF, valid), 1.0, 0.0)
    z = jnp.ones_like(tops[0])
    for m in tops[1:]:
        z = z + jnp.exp(m - tops[0])
    n1 = jnp.zeros(s1.shape, F32)
    total = jnp.zeros_like(z)
    for a in range(topk):
        if a == 0:
            n_a = jnp.sum(taken[0:topk, :], axis=0, keepdims=True)
        elif a < sub:
            base = topk + sub * (a - 1)
            n_a = jnp.sum(taken[base:base + sub, :], axis=0, keepdims=True)
        else:
            row = topk + sub * (sub - 1) + (a - sub)
            n_a = taken[row:row + 1, :]
        total = total + n_a
        n1 = jnp.where((s1 == v1[a]) if idx is None else (rank1 == float(a)), n_a, n1)
    if idx is None:
        k = float(topk)
        c1 = jnp.sum(jnp.where(cur1 == NEG_INF, 1.0, 0.0), axis=0, keepdims=True)
        c2 = jnp.sum(jnp.where(cur2 == NEG_INF, 1.0, 0.0), axis=0, keepdims=True)
        bad = jnp.where(jnp.logical_and(jnp.logical_and(c1 == k, c2 == k), total == k), 0.0, 1.0)
    else:
        bad = None
    e2 = jnp.exp(s2 - v2[0])
    e1 = jnp.exp(s1 - v1[0]) / z
    return rank2, e2, n1, e1, bad


def _peer_select_kernel(s_ref, r2_ref, e2_ref, n1_ref, e1_ref, *, heads, topk):
    sub = V7X_SUBLANES
    nk = s_ref.shape[1]
    lanes = s_ref.shape[2]
    ncand = topk + sub * sub
    idx = lax.broadcasted_iota(jnp.int32, (nk, lanes), 0).astype(F32)
    crow = lax.broadcasted_iota(jnp.int32, (ncand, lanes), 0)
    cidx = crow.astype(F32)
    ridx_k = lax.broadcasted_iota(jnp.int32, (topk, lanes), 0).astype(F32)
    ridx_s = lax.broadcasted_iota(jnp.int32, (sub, lanes), 0).astype(F32)
    valid = crow < topk + sub
    for a in range(2, sub):
        base = topk + sub * (a - 1)
        valid = jnp.logical_or(valid, jnp.logical_and(crow >= base, crow < base + topk // (a + 1)))
    valid = jnp.logical_or(valid, crow >= topk + sub * (sub - 1))

    def write(h, res):
        rank2, e2, n1, e1 = res
        r2_ref[h] = rank2.astype(r2_ref.dtype)
        e2_ref[h] = e2.astype(e2_ref.dtype)
        n1_ref[h] = n1
        e1_ref[h] = e1

    def head_pair(hp, carry):
        hs = (2 * hp, 2 * hp + 1)
        sc = [(s_ref[2 * h], s_ref[2 * h + 1]) for h in hs]
        bad = None
        for h, (s1, s2) in zip(hs, sc):
            *res, b = _peer_select_one(s1, s2, topk, None, None, valid, ridx_k, ridx_s)
            write(h, res)
            bad = b if bad is None else jnp.maximum(bad, b)

        @pl.when(jnp.max(bad) > 0.0)
        def _():
            for h, (s1, s2) in zip(hs, sc):
                *res, _ = _peer_select_one(s1, s2, topk, idx, cidx, valid, ridx_k, ridx_s)
                write(h, res)

        return carry

    lax.fori_loop(0, heads // 2, head_pair, 0)


def _peer_select(scores, heads, topk):
    nkh, nk, t = scores.shape
    tl = _row_tile(t, 2 * V7X_LANES)
    shp = jax.ShapeDtypeStruct((heads, nk, t), F32)
    shp_i2 = jax.ShapeDtypeStruct((heads, nk, t), BF16)
    ospec = pl.BlockSpec((heads, nk, tl), lambda i: (0, 0, i))
    return pl.pallas_call(
        functools.partial(_peer_select_kernel, heads=heads, topk=topk),
        out_shape=[shp_i2, shp_i2, shp, shp],
        grid=(t // tl,),
        in_specs=[pl.BlockSpec((nkh, nk, tl), lambda i: (0, 0, i))],
        out_specs=[ospec, ospec, ospec, ospec],
        compiler_params=_cparams("parallel"),
        name="peer_select",
    )(scores)


def _peer_dense_kernel(ht_ref, u_ref, vt_ref, r2_ref, e2_ref, n1_ref, e1_ref, x_ref, gate_ref, o_ref,
                       at0_scr, at1_scr, wt_scr, acc_scr, *, heads, nk, i1_per_step, nblk):
    j = pl.program_id(1)
    at_scr = (at0_scr, at1_scr)

    def activations(dst):
        dst[...] = _dot(u_ref[...], ht_ref[...])

    pack = 2 * V7X_SUBLANES
    tn = wt_scr.shape[1]

    def packed_row(ref, h, r):
        return jnp.broadcast_to(ref[h, r:r + 1, :], (pack, tn)).astype(BF16)[None]

    def mix(src):
        for r in range(i1_per_step):
            rows = pl.ds(r * nk, nk)
            g = None
            for h in range(heads):
                n = packed_row(n1_ref, h, r)
                e1 = packed_row(e1_ref, h, r)
                r2 = r2_ref[h].reshape(nk // pack, pack, tn)
                e2 = e2_ref[h].reshape(nk // pack, pack, tn)
                term = jnp.where(r2 < n, e2, jnp.zeros_like(e2)) * e1
                g = term if g is None else g + term
            g = g.reshape(nk, tn).astype(F32)
            wt_scr[rows, :] = (_gelu_tanh(src[rows, :]) * g).astype(BF16)
        acc_scr[...] += _dot(vt_ref[0], wt_scr[...])

    @pl.when(j == 0)
    def _():
        acc_scr[...] = jnp.zeros_like(acc_scr)
        activations(at_scr[0])

    for parity in range(2):
        @pl.when(jnp.logical_and(jnp.logical_and(j > 0, j < nblk), j % 2 == parity))
        def _():
            activations(at_scr[parity])
            mix(at_scr[1 - parity])

    @pl.when(j == nblk)
    def _():
        mix(at_scr[(nblk - 1) % 2])
        o_ref[...] = x_ref[...] + gate_ref[0] * jnp.transpose(acc_scr[...])


def _peer_dense(ht, u, vt, r2, e2, n1, e1, x, gate, seq):
    t, d = x.shape
    heads, nk, _ = r2.shape
    i1_per_step = V7X_SUBLANES
    nblk, _, te = vt.shape
    assert te == i1_per_step * nk and u.shape == (nblk * te, d)
    tn = _row_tile(min(t, seq) if seq > 1 else t, 512)
    clamp = lambda b: jnp.clip(b, 0, nblk - 1)
    tok = pl.BlockSpec((heads, nk, tn), lambda i, j: (0, 0, i))
    per_i1 = pl.BlockSpec((heads, i1_per_step, tn), lambda i, j: (0, clamp(j - 1), i))
    return pl.pallas_call(
        functools.partial(_peer_dense_kernel, heads=heads, nk=nk, i1_per_step=i1_per_step, nblk=nblk),
        out_shape=jax.ShapeDtypeStruct((t, d), F32),
        grid=(t // tn, nblk + 1),
        in_specs=[
            pl.BlockSpec((d, tn), lambda i, j: (0, i)),
            pl.BlockSpec((te, d), lambda i, j: (clamp(j), 0)),
            pl.BlockSpec((1, d, te), lambda i, j: (clamp(j - 1), 0, 0)),
            tok, tok, per_i1, per_i1,
            pl.BlockSpec((tn, d), lambda i, j: (i, 0)),
            _mod_specs(tn, d, seq),
        ],
        out_specs=pl.BlockSpec((tn, d), lambda i, j: (i, 0)),
        scratch_shapes=[pltpu.VMEM((te, tn), F32), pltpu.VMEM((te, tn), F32),
                        pltpu.VMEM((te, tn), BF16), pltpu.VMEM((d, tn), F32)],
        compiler_params=_cparams("parallel", "arbitrary"),
        name="peer_dense",
    )(ht, u, vt, r2, e2, n1, e1, x, gate)


def _rmsnorm_kernel(x_ref, g_ref, o_ref):
    x = x_ref[...]
    o_ref[...] = (x * lax.rsqrt(jnp.mean(x * x, axis=-1, keepdims=True) + EPS)) * g_ref[...]


def _rmsnorm(x, g):
    t, d = x.shape
    tm = _row_tile(t, 1024)
    return pl.pallas_call(
        _rmsnorm_kernel,
        out_shape=jax.ShapeDtypeStruct((t, d), F32),
        grid=(t // tm,),
        in_specs=[pl.BlockSpec((tm, d), lambda i: (i, 0)), pl.BlockSpec((1, d), lambda i: (0, 0))],
        out_specs=pl.BlockSpec((tm, d), lambda i: (i, 0)),
        compiler_params=_cparams("parallel"),
        name="final_rmsnorm",
    )(x, g)


def _mods(mod, lo, hi, seq):
    d = mod.shape[1] // 6
    rows = mod[lo:hi]
    parts = [rows[:, k * d:(k + 1) * d] for k in range(6)]
    if seq == 1:
        return [p[None, :, :] for p in parts]
    return [p[:, None, :] for p in parts]


def kernel(x_prompt, x_sample, c_prompt, c_sample, state_ret, state_ssm_re, state_ssm_im, norm_g, final_g, w_ada, b_ada, ret_w_in, ret_w_out, ssm_w_in, ssm_lam_re, ssm_lam_im, ssm_log_dt, ssm_b_re, ssm_b_im, ssm_c_re, ssm_c_im, ssm_d, ssm_w_glu, peer_w_q, peer_key1, peer_key2, peer_u, peer_v):
    batch, seq, d = x_prompt.shape
    dbatch, dseq, _ = x_sample.shape
    assert dseq == 1 and seq % RET_CHUNK == 0
    depth = w_ada.shape[0]
    heads = peer_key1.shape[1]

    c_all = jnp.concatenate([c_prompt, c_sample], axis=0)
    pad = (-c_all.shape[0]) % V7X_SUBLANES
    c_all = jnp.pad(c_all, ((0, pad), (0, 0)))
    mod = _ada(c_all, w_ada, b_ada)

    groups = [
        dict(x=x_prompt.reshape(batch * seq, d), lo=0, hi=batch, seq=seq, nb=batch),
        dict(x=x_sample.reshape(dbatch, d), lo=batch, hi=batch + dbatch, seq=1, nb=dbatch),
    ]
    outs = [dict(ret=[], re=[], im=[]) for _ in groups]
    ret_sample = None

    for i in range(depth):
        jm = i // N_MIXERS
        is_ret = i % N_MIXERS == 0
        w_q = peer_w_q[i].astype(BF16)
        keys = jnp.stack([peer_key1[i], peer_key2[i]], axis=1).reshape(2 * heads, PEER_NKEYS, -1).astype(BF16)
        u_tab = peer_u[i].astype(BF16)
        te = V7X_SUBLANES * PEER_NKEYS
        vt_tab = jnp.transpose(peer_v[i].reshape(-1, te, d), (0, 2, 1)).astype(BF16)
        g_mix = norm_g[i, 0].reshape(1, d)
        g_peer = norm_g[i, 1].reshape(1, d)
        if is_ret:
            w_in = ret_w_in[jm].astype(BF16)
            w_out = ret_w_out[jm].astype(BF16)
        else:
            w_in = ssm_w_in[jm].astype(BF16)
            w_out = ssm_w_glu[jm].astype(BF16)
            a_re, a_im, bbr, bbi = _ssm_discretize(ssm_lam_re[jm], ssm_lam_im[jm], ssm_log_dt[jm],
                                                   ssm_b_re[jm], ssm_b_im[jm])
            per = V7X_MXU_DIM // SSM_GROUP
            disc = (_block_diag(bbr, per).astype(BF16), _block_diag(bbi, per).astype(BF16),
                    _block_diag(jnp.transpose(ssm_c_re[jm], (0, 2, 1)), per).astype(BF16),
                    _block_diag(jnp.transpose(ssm_c_im[jm], (0, 2, 1)), per).astype(BF16),
                    ssm_d[jm].reshape(1, d), a_re.reshape(1, -1), a_im.reshape(1, -1))

        for gi, grp in enumerate(groups):
            x = grp["x"]
            sq = grp["seq"]
            sh1, sc1, g1, sh2, sc2, g2 = _mods(mod[i], grp["lo"], grp["hi"], sq)
            tmaj = sq > 1 and not is_ret
            proj = _nm_matmul(x, g_mix, sh1, sc1, w_in, sq, time_major=tmaj)
            if is_ret:
                if sq > 1:
                    y, s = _retention_prompt(proj, grp["nb"], sq, d)
                    outs[gi]["ret"].append(s)
                else:
                    y, ret_sample = _retention_step(proj, state_ret, jm, ret_sample, d)
                x = _matmul_res(y, w_out, x, g1, sq, glu=False)
            else:
                if sq > 1:
                    z, fr, fi = _ssm_prompt(proj, disc, grp["nb"], sq)
                else:
                    z, fr, fi = _ssm_step(proj, state_ssm_re[jm].reshape(dbatch, -1),
                                          state_ssm_im[jm].reshape(dbatch, -1), disc)
                outs[gi]["re"].append(fr.reshape(grp["nb"], -1, SSM_STATE))
                outs[gi]["im"].append(fi.reshape(grp["nb"], -1, SSM_STATE))
                x = _matmul_res(z, w_out, x, g1, sq, glu=True, a_time_major=tmaj)
            scores, h2 = _peer_scores(x, g_peer, sh2, sc2, w_q, keys, sq)
            r2, e2, n1, e1 = _peer_select(scores, heads, PEER_TOPK)
            x = _peer_dense(h2, u_tab, vt_tab, r2, e2, n1, e1, x, g2, sq)
            grp["x"] = x

    fg = final_g.reshape(1, d)
    y_prompt = _rmsnorm(groups[0]["x"], fg).reshape(batch, seq, d)
    y_sample = _rmsnorm(groups[1]["x"], fg).reshape(dbatch, 1, d)
    return (y_prompt, y_sample,
            jnp.stack(outs[0]["ret"]), ret_sample,
            jnp.stack(outs[0]["re"]), jnp.stack(outs[0]["im"]),
            jnp.stack(outs[1]["re"]), jnp.stack(outs[1]["im"]))
```

```python
import functools
import math

import jax
import jax.numpy as jnp
from jax import lax
from jax.experimental import pallas as pl
from jax.experimental.pallas import tpu as pltpu

F32 = jnp.float32
BF16 = jnp.bfloat16

EPS = 1e-6
ROPE_BASE = 10000.0
PAST_LEN = 16384
RET_HEADS = 4
RET_CHUNK = 128
SSM_GROUP = 16
SSM_STATE = 64
PEER_HEADS = 8
PEER_NKEYS = 128
PEER_TOPK = 16
N_MIXERS = 2

V7X_LANES = 128
V7X_SUBLANES = 8
V7X_MXU_DIM = 256
V7X_VMEM_LIMIT_BYTES = 56 * 1024 * 1024

NEG_INF = float("-inf")


def _cparams(*sem):
    return pltpu.CompilerParams(dimension_semantics=sem, vmem_limit_bytes=V7X_VMEM_LIMIT_BYTES)


def _dot(a, b):
    return jnp.dot(a, b, preferred_element_type=F32)


def _dot_nt(a, b):
    return lax.dot_general(a, b, (((1,), (1,)), ((), ())), preferred_element_type=F32)


def _dot_tn(a, b):
    return lax.dot_general(a, b, (((0,), (0,)), ((), ())), preferred_element_type=F32)


def _sigmoid(x):
    return 1.0 / (1.0 + jnp.exp(-x))


def _gelu_tanh(x):
    c = math.sqrt(2.0 / math.pi)
    hx = 0.5 * x
    return hx + hx * jnp.tanh(x * (c + (c * 0.044715) * (x * x)))


def _row_tile(t, pref):
    if t <= pref:
        return t
    tile = pref
    while t % tile:
        tile //= 2
    return tile


def _ada_kernel(c_ref, w_ref, b_ref, o_ref):
    c = c_ref[...]
    sc = (c * _sigmoid(c)).astype(BF16)
    o_ref[0] = _dot(sc, w_ref[0].astype(BF16)) + b_ref[0]


def _ada(c_all, w_ada, b_ada):
    m, d = c_all.shape
    depth, _, n = w_ada.shape
    tn = 1024
    return pl.pallas_call(
        _ada_kernel,
        out_shape=jax.ShapeDtypeStruct((depth, m, n), F32),
        grid=(depth, n // tn),
        in_specs=[
            pl.BlockSpec((m, d), lambda l, j: (0, 0)),
            pl.BlockSpec((1, d, tn), lambda l, j: (l, 0, j)),
            pl.BlockSpec((1, 1, tn), lambda l, j: (l, 0, j)),
        ],
        out_specs=pl.BlockSpec((1, m, tn), lambda l, j: (l, 0, j)),
        compiler_params=_cparams("parallel", "parallel"),
        name="ada_mod",
    )(c_all, w_ada, b_ada.reshape(depth, 1, n))


def _norm_mod(x, g, sh, sc):
    y = x * lax.rsqrt(jnp.mean(x * x, axis=-1, keepdims=True) + EPS)
    return (y * g) * (1.0 + sc) + sh


def _nm_matmul_kernel(x_ref, g_ref, sh_ref, sc_ref, w_ref, o_ref, h_scr):
    @pl.when(pl.program_id(1) == 0)
    def _():
        h_scr[...] = _norm_mod(x_ref[...], g_ref[...], sh_ref[0], sc_ref[0]).astype(BF16)

    o_ref[...] = _dot(h_scr[...], w_ref[...])


def _mod_specs(tm, d, seq):
    if seq == 1:
        return pl.BlockSpec((1, tm, d), lambda i, *_: (0, i, 0))
    per = seq // tm
    return pl.BlockSpec((1, 1, d), lambda i, *_: (i // per, 0, 0))


def _nm_matmul(x, g, sh, sc, w, seq, time_major=False):
    t, d = x.shape
    n = w.shape[1]
    tm = _row_tile(min(t, seq) if seq > 1 else t, 512)
    tn = min(n, 1024)
    if time_major:
        per, nj = seq // tm, n // tn
        out_shape = jax.ShapeDtypeStruct((seq, (t // seq) * n), F32)
        out_spec = pl.BlockSpec((tm, tn), lambda i, j: (i % per, (i // per) * nj + j))
    else:
        out_shape = jax.ShapeDtypeStruct((t, n), F32)
        out_spec = pl.BlockSpec((tm, tn), lambda i, j: (i, j))
    res = pl.pallas_call(
        _nm_matmul_kernel,
        out_shape=out_shape,
        grid=(t // tm, n // tn),
        in_specs=[
            pl.BlockSpec((tm, d), lambda i, j: (i, 0)),
            pl.BlockSpec((1, d), lambda i, j: (0, 0)),
            _mod_specs(tm, d, seq),
            _mod_specs(tm, d, seq),
            pl.BlockSpec((d, tn), lambda i, j: (0, j)),
        ],
        out_specs=out_spec,
        scratch_shapes=[pltpu.VMEM((tm, d), BF16)],
        compiler_params=_cparams("parallel", "arbitrary"),
        name="norm_mod_matmul",
    )(x, g, sh, sc, w)
    return res.reshape(t, n) if time_major else res


def _matmul_res_kernel(a_ref, w_ref, x_ref, gate_ref, o_ref, *, glu):
    r = _dot(a_ref[...], w_ref[...])
    if glu:
        half = r.shape[1] // 2
        r = r[:, :half] * _sigmoid(r[:, half:])
    o_ref[...] = x_ref[...] + gate_ref[0] * r


def _matmul_res(a, w, x, gate, seq, glu, a_time_major=False, tm_pref=512):
    t, k = a.shape
    n = w.shape[1]
    d = x.shape[1]
    tm = _row_tile(min(t, seq) if seq > 1 else t, tm_pref)
    if a_time_major:
        per = seq // tm
        a = a.reshape(seq, (t // seq) * k)
        a_spec = pl.BlockSpec((tm, k), lambda i: (i % per, i // per))
    else:
        a_spec = pl.BlockSpec((tm, k), lambda i: (i, 0))
    return pl.pallas_call(
        functools.partial(_matmul_res_kernel, glu=glu),
        out_shape=jax.ShapeDtypeStruct((t, d), F32),
        grid=(t // tm,),
        in_specs=[
            a_spec,
            pl.BlockSpec((k, n), lambda i: (0, 0)),
            pl.BlockSpec((tm, d), lambda i: (i, 0)),
            _mod_specs(tm, d, seq),
        ],
        out_specs=pl.BlockSpec((tm, d), lambda i: (i, 0)),
        compiler_params=_cparams("parallel"),
        name="matmul_residual",
    )(a, w, x, gate)


def _rotary(x, cos, sin):
    half = x.shape[-1] // 2
    x1 = x[:, :half]
    x2 = x[:, half:]
    return jnp.concatenate([x1 * cos - x2 * sin, x2 * cos + x1 * sin], axis=-1)


def _group_norm_gate(o, g):
    mu = jnp.mean(o, axis=-1, keepdims=True)
    var = jnp.mean(jnp.square(o - mu), axis=-1, keepdims=True)
    on = (o - mu) * lax.rsqrt(var + EPS)
    return (g * _sigmoid(g)) * on


def _ret_chunk_kernel(q_ref, k_ref, v_ref, g_ref, cos_ref, sin_ref, dmask_ref, cross_ref, tail_ref,
                      cdec_ref, o_ref, sfin_ref, s_scr, *, n_sub, chunk, k_scale):
    c = pl.program_id(2)

    @pl.when(c == 0)
    def _():
        s_scr[...] = jnp.zeros_like(s_scr)

    dmask = dmask_ref[0]
    cross = cross_ref[0]
    tail = tail_ref[0]
    cdec = cdec_ref[0]
    for j in range(n_sub):
        rows = pl.ds(j * chunk, chunk)
        cos = cos_ref[rows, :]
        sin = sin_ref[rows, :]
        qc = _rotary(q_ref[rows, :], cos, sin)
        kc = _rotary(k_ref[rows, :], cos, sin) * k_scale
        vb = v_ref[rows, :].astype(BF16)
        qb = qc.astype(BF16)
        att = _dot_nt(qb, kc.astype(BF16)) * dmask
        s = s_scr[...]
        o = _dot(att.astype(BF16), vb) + _dot(qb, s.astype(BF16)) * cross
        s_scr[...] = s * cdec + _dot_tn((kc * tail).astype(BF16), vb)
        o_ref[rows, :] = _group_norm_gate(o, g_ref[rows, :]).astype(BF16)

    @pl.when(c == pl.num_programs(2) - 1)
    def _():
        sfin_ref[0, 0] = s_scr[...]


def _ret_tables(heads, chunk):
    lg = jnp.log1p(-jnp.exp2(-5.0 - jnp.arange(heads, dtype=F32)))
    i = jnp.arange(chunk, dtype=F32)
    diff = i[:, None] - i[None, :]
    dmask = jnp.where(diff >= 0, jnp.exp(lg[:, None, None] * jnp.maximum(diff, 0.0)), 0.0)
    cross = jnp.exp(lg[:, None] * (i[None, :] + 1.0))[:, :, None]
    tail = jnp.exp(lg[:, None] * (chunk - 1.0 - i[None, :]))[:, :, None]
    cdec = jnp.exp(lg * chunk)
    return dmask, cross, tail, cdec


def _rope_tables(pos, half):
    inv = ROPE_BASE ** (-jnp.arange(half, dtype=F32) / half)
    ang = pos.astype(F32)[:, None] * inv[None, :]
    return jnp.cos(ang), jnp.sin(ang)


def _retention_prompt(proj, batch, seq, d):
    heads = RET_HEADS
    dk = d // heads
    dv = 2 * dk
    chunk = RET_CHUNK
    rows = _row_tile(seq, 4 * chunk)
    n_sub = rows // chunk
    nblk = seq // rows
    cos, sin = _rope_tables(jnp.arange(seq, dtype=jnp.int32), dk // 2)
    dmask, cross, tail, cdec = _ret_tables(heads, chunk)
    cdec = jnp.broadcast_to(cdec[:, None, None], (heads, 1, dv))
    kq = heads
    vq = (2 * heads * dk) // dv
    gq = vq + heads
    o, sfin = pl.pallas_call(
        functools.partial(_ret_chunk_kernel, n_sub=n_sub, chunk=chunk, k_scale=dk ** -0.5),
        out_shape=[jax.ShapeDtypeStruct((batch * seq, heads * dv), BF16),
                   jax.ShapeDtypeStruct((batch, heads, dk, dv), F32)],
        grid=(batch, heads, nblk),
        in_specs=[
            pl.BlockSpec((rows, dk), lambda b, h, c: (b * nblk + c, h)),
            pl.BlockSpec((rows, dk), lambda b, h, c: (b * nblk + c, kq + h)),
            pl.BlockSpec((rows, dv), lambda b, h, c: (b * nblk + c, vq + h)),
            pl.BlockSpec((rows, dv), lambda b, h, c: (b * nblk + c, gq + h)),
            pl.BlockSpec((rows, dk // 2), lambda b, h, c: (c, 0)),
            pl.BlockSpec((rows, dk // 2), lambda b, h, c: (c, 0)),
            pl.BlockSpec((1, chunk, chunk), lambda b, h, c: (h, 0, 0)),
            pl.BlockSpec((1, chunk, 1), lambda b, h, c: (h, 0, 0)),
            pl.BlockSpec((1, chunk, 1), lambda b, h, c: (h, 0, 0)),
            pl.BlockSpec((1, 1, dv), lambda b, h, c: (h, 0, 0)),
        ],
        out_specs=[
            pl.BlockSpec((rows, dv), lambda b, h, c: (b * nblk + c, h)),
            pl.BlockSpec((1, 1, dk, dv), lambda b, h, c: (b, h, 0, 0)),
        ],
        scratch_shapes=[pltpu.VMEM((dk, dv), F32)],
        compiler_params=_cparams("parallel", "parallel", "arbitrary"),
        name="retention_chunks",
    )(proj, proj, proj, proj, cos, sin, dmask, cross, tail, cdec)
    return o, sfin


def _ret_step_kernel(q_ref, k_ref, v_ref, g_ref, cos_ref, sin_ref, gam_ref, s0_ref, *rest, nb, k_scale,
                     layer, copy_others):
    o_ref, s_ref = rest[-2:]
    if copy_others:
        @pl.when(pl.program_id(2) != layer)
        def _():
            s_ref[...] = s0_ref[...]

        pl.when(pl.program_id(2) == layer)(
            functools.partial(_ret_step_body, q_ref, k_ref, v_ref, g_ref, cos_ref, sin_ref, gam_ref, s0_ref,
                              o_ref, s_ref, nb, k_scale))
    else:
        _ret_step_body(q_ref, k_ref, v_ref, g_ref, cos_ref, sin_ref, gam_ref, s0_ref, o_ref, s_ref, nb, k_scale)


def _ret_step_body(q_ref, k_ref, v_ref, g_ref, cos_ref, sin_ref, gam_ref, s0_ref, o_ref, s_ref, nb, k_scale):
    cos = cos_ref[...]
    sin = sin_ref[...]
    gam = gam_ref[0]
    qc = _rotary(q_ref[...], cos, sin)
    kc = _rotary(k_ref[...], cos, sin) * k_scale
    qr = qc.astype(BF16).astype(F32)
    kr = kc.astype(BF16).astype(F32)
    vr = v_ref[...].astype(BF16).astype(F32)
    att = jnp.sum(qr * kr, axis=-1, keepdims=True)
    att = att.astype(BF16).astype(F32)
    dk = qc.shape[1]
    eye = (lax.broadcasted_iota(jnp.int32, (dk, dk), 0) == lax.broadcasted_iota(jnp.int32, (dk, dk), 1))
    rows8 = lax.broadcasted_iota(jnp.int32, (V7X_SUBLANES, dk), 0)
    o = jnp.zeros(vr.shape, F32)
    brow = lax.broadcasted_iota(jnp.int32, vr.shape, 0)
    for b in range(nb):
        s0 = s0_ref[0, b, 0]
        qrow = qc[b:b + 1, :]
        q8 = jnp.where(rows8 == 0, qrow, 0.0).astype(BF16)
        qs = _dot(q8, s0.astype(BF16))[0:1, :]
        o = jnp.where(brow == b, att[b:b + 1, :] * vr[b:b + 1, :] + qs * gam, o)
        kcol = jnp.sum(jnp.where(eye, kr[b:b + 1, :], 0.0), axis=1, keepdims=True)
        s_ref[0, b, 0] = s0 * gam + kcol * vr[b:b + 1, :]
    o_ref[...] = _group_norm_gate(o, g_ref[...]).astype(BF16)


def _retention_step(proj, states, layer, new_states, d):
    t = proj.shape[0]
    heads = RET_HEADS
    dk = d // heads
    dv = 2 * dk
    nb = V7X_SUBLANES
    cos, sin = _rope_tables(PAST_LEN + jnp.arange(1, dtype=jnp.int32), dk // 2)
    lg = jnp.log1p(-jnp.exp2(-5.0 - jnp.arange(heads, dtype=F32)))
    gam = jnp.broadcast_to(jnp.exp(lg)[:, None, None], (heads, 1, dv))
    kq = heads
    vq = (2 * heads * dk) // dv
    gq = vq + heads
    first = new_states is None
    if first:
        grid = (t // nb, heads, states.shape[0])
        state_spec = pl.BlockSpec((1, nb, 1, dk, dv), lambda i, h, l: (l, i, h, 0, 0))
        sem = ("parallel", "parallel", "arbitrary")
    else:
        grid = (t // nb, heads)
        state_spec = pl.BlockSpec((1, nb, 1, dk, dv), lambda i, h: (layer, i, h, 0, 0))
        sem = ("parallel", "parallel")
    in_specs = [
        pl.BlockSpec((nb, dk), lambda i, h, *_: (i, h)),
        pl.BlockSpec((nb, dk), lambda i, h, *_: (i, kq + h)),
        pl.BlockSpec((nb, dv), lambda i, h, *_: (i, vq + h)),
        pl.BlockSpec((nb, dv), lambda i, h, *_: (i, gq + h)),
        pl.BlockSpec((1, dk // 2), lambda i, h, *_: (0, 0)),
        pl.BlockSpec((1, dk // 2), lambda i, h, *_: (0, 0)),
        pl.BlockSpec((1, 1, dv), lambda i, h, *_: (h, 0, 0)),
        state_spec,
    ]
    args = [proj, proj, proj, proj, cos, sin, gam, states]
    aliases = {}
    if not first:
        in_specs.append(pl.BlockSpec(memory_space=pl.ANY))
        args.append(new_states)
        aliases = {len(args) - 1: 1}
    o, s = pl.pallas_call(
        functools.partial(_ret_step_kernel, nb=nb, k_scale=dk ** -0.5, layer=layer, copy_others=first),
        out_shape=[jax.ShapeDtypeStruct((t, heads * dv), BF16),
                   jax.ShapeDtypeStruct(states.shape, F32)],
        grid=grid,
        in_specs=in_specs,
        out_specs=[pl.BlockSpec((nb, dv), lambda i, h, *_: (i, h)), state_spec],
        input_output_aliases=aliases,
        compiler_params=_cparams(*sem),
        name="retention_step",
    )(*args)
    return o, s


def _ssm_disc_kernel(lr_ref, li_ref, ldt_ref, br_ref, bi_ref, are_ref, aim_ref, bbr_ref, bbi_ref):
    lr = lr_ref[...]
    li = li_ref[...]
    dt = jnp.exp(ldt_ref[...])
    mag = jnp.exp(lr * dt)
    a_re = mag * jnp.cos(li * dt)
    a_im = mag * jnp.sin(li * dt)
    den = lr * lr + li * li
    nr = a_re - 1.0
    c_re = (nr * lr + a_im * li) / den
    c_im = (a_im * lr - nr * li) / den
    are_ref[...] = a_re
    aim_ref[...] = a_im
    br = br_ref[...]
    bi = bi_ref[...]
    bbr_ref[...] = c_re * br - c_im * bi
    bbi_ref[...] = c_re * bi + c_im * br


def _ssm_discretize(lam_re, lam_im, log_dt, b_re, b_im):
    g, n = lam_re.shape
    p = b_re.shape[2]
    brt = jnp.transpose(b_re, (0, 2, 1))
    bit = jnp.transpose(b_im, (0, 2, 1))
    a_re, a_im, bbr, bbi = pl.pallas_call(
        _ssm_disc_kernel,
        out_shape=[jax.ShapeDtypeStruct((g, 1, n), F32), jax.ShapeDtypeStruct((g, 1, n), F32),
                   jax.ShapeDtypeStruct((g, p, n), F32), jax.ShapeDtypeStruct((g, p, n), F32)],
        name="ssm_discretize",
    )(lam_re.reshape(g, 1, n), lam_im.reshape(g, 1, n), log_dt.reshape(g, 1, 1), brt, bit)
    return a_re.reshape(g, n), a_im.reshape(g, n), bbr, bbi


def _block_diag(w, per):
    g, a, b = w.shape
    w = w.reshape(g // per, per, a, b)
    eye = jnp.eye(per, dtype=w.dtype)
    return jnp.einsum('cgab,gh->cgahb', w, eye).reshape(g // per, per * a, per * b)


def _ssm_bu(ub, bdr_ref, bdi_ref, nblk, kw):
    bur = jnp.concatenate([_dot(ub[:, c * kw:(c + 1) * kw], bdr_ref[c]) for c in range(nblk)], axis=1)
    bui = jnp.concatenate([_dot(ub[:, c * kw:(c + 1) * kw], bdi_ref[c]) for c in range(nblk)], axis=1)
    return bur, bui


def _ssm_y(hr, hi, cdr_ref, cdi_ref, nblk, kw):
    hrb = hr.astype(BF16)
    hib = hi.astype(BF16)
    return jnp.concatenate(
        [_dot(hrb[:, c * kw:(c + 1) * kw], cdr_ref[c]) - _dot(hib[:, c * kw:(c + 1) * kw], cdi_ref[c])
         for c in range(nblk)], axis=1)


def _ssm_scan_kernel(u_ref, bdr_ref, bdi_ref, cdr_ref, cdi_ref, d_ref, ar_ref, ai_ref,
                     z_ref, fr_ref, fi_ref, hr_scr, hi_scr, cr_scr, ci_scr, *, nblk, nb, lane_chunk):
    step = pl.program_id(0)

    @pl.when(step == 0)
    def _():
        cr_scr[...] = jnp.zeros_like(cr_scr)
        ci_scr[...] = jnp.zeros_like(ci_scr)

    u = u_ref[...]
    kw_in = u.shape[1] // nblk
    bur, bui = _ssm_bu(u.astype(BF16), bdr_ref, bdi_ref, nblk, kw_in)
    hr_scr[...] = bur
    hi_scr[...] = bui

    rows_total, gn = hr_scr.shape
    for c in range(gn // lane_chunk):
        lanes = pl.ds(c * lane_chunk, lane_chunk)
        ar = jnp.broadcast_to(ar_ref[:, lanes], (nb, lane_chunk))
        ai = jnp.broadcast_to(ai_ref[:, lanes], (nb, lane_chunk))

        def time_step(s, carry):
            hr, hi = carry
            rows = pl.ds(pl.multiple_of(s * nb, nb), nb)
            nr = hr_scr[rows, lanes] + (ar * hr - ai * hi)
            ni = hi_scr[rows, lanes] + (ar * hi + ai * hr)
            hr_scr[rows, lanes] = nr
            hi_scr[rows, lanes] = ni
            return nr, ni

        hr, hi = lax.fori_loop(0, rows_total // nb, time_step, (cr_scr[:, lanes], ci_scr[:, lanes]))
        cr_scr[:, lanes] = hr
        ci_scr[:, lanes] = hi

    kw_out = gn // nblk
    y = _ssm_y(hr_scr[...], hi_scr[...], cdr_ref, cdi_ref, nblk, kw_out) + d_ref[...] * u
    z_ref[...] = _gelu_tanh(y).astype(BF16)

    @pl.when(step == pl.num_programs(0) - 1)
    def _():
        fr_ref[...] = cr_scr[...]
        fi_ref[...] = ci_scr[...]


def _ssm_prompt(u_tm, disc, batch, seq):
    t, d = u_tm.shape
    bdr, bdi, cdr, cdi, dvec, a_re, a_im = disc
    nblk = bdr.shape[0]
    gn = a_re.shape[1]
    assert batch == V7X_SUBLANES, "one time step of all sequences must fill the sublanes of a vreg"
    rows = batch * _row_tile(seq, 32)
    full = lambda a: pl.BlockSpec(a.shape, lambda s: (0,) * a.ndim)
    z, fr, fi = pl.pallas_call(
        functools.partial(_ssm_scan_kernel, nblk=nblk, nb=batch, lane_chunk=4 * V7X_LANES),
        out_shape=[jax.ShapeDtypeStruct((t, d), BF16),
                   jax.ShapeDtypeStruct((batch, gn), F32),
                   jax.ShapeDtypeStruct((batch, gn), F32)],
        grid=(t // rows,),
        in_specs=[pl.BlockSpec((rows, d), lambda s: (s, 0)),
                  full(bdr), full(bdi), full(cdr), full(cdi), full(dvec), full(a_re), full(a_im)],
        out_specs=[pl.BlockSpec((rows, d), lambda s: (s, 0)),
                   pl.BlockSpec((batch, gn), lambda s: (0, 0)),
                   pl.BlockSpec((batch, gn), lambda s: (0, 0))],
        scratch_shapes=[pltpu.VMEM((rows, gn), F32), pltpu.VMEM((rows, gn), F32),
                        pltpu.VMEM((batch, gn), F32), pltpu.VMEM((batch, gn), F32)],
        compiler_params=_cparams("arbitrary"),
        name="ssm_scan",
    )(u_tm, bdr, bdi, cdr, cdi, dvec, a_re, a_im)
    return z, fr, fi


def _ssm_step_kernel(u_ref, h0r_ref, h0i_ref, bdr_ref, bdi_ref, cdr_ref, cdi_ref, d_ref, ar_ref, ai_ref,
                     z_ref, fr_ref, fi_ref, *, nblk):
    u = u_ref[...]
    kw_in = u.shape[1] // nblk
    bur, bui = _ssm_bu(u.astype(BF16), bdr_ref, bdi_ref, nblk, kw_in)
    ar = ar_ref[...]
    ai = ai_ref[...]
    h0r = h0r_ref[...]
    h0i = h0i_ref[...]
    hr = bur + (ar * h0r - ai * h0i)
    hi = bui + (ar * h0i + ai * h0r)
    fr_ref[...] = hr
    fi_ref[...] = hi
    kw_out = hr.shape[1] // nblk
    y = _ssm_y(hr, hi, cdr_ref, cdi_ref, nblk, kw_out) + d_ref[...] * u
    z_ref[...] = _gelu_tanh(y).astype(BF16)


def _ssm_step(u, h0r, h0i, disc):
    t, d = u.shape
    bdr, bdi, cdr, cdi, dvec, a_re, a_im = disc
    gn = a_re.shape[1]
    return pl.pallas_call(
        functools.partial(_ssm_step_kernel, nblk=bdr.shape[0]),
        out_shape=[jax.ShapeDtypeStruct((t, d), BF16),
                   jax.ShapeDtypeStruct((t, gn), F32), jax.ShapeDtypeStruct((t, gn), F32)],
        name="ssm_step",
        compiler_params=pltpu.CompilerParams(vmem_limit_bytes=V7X_VMEM_LIMIT_BYTES),
    )(u, h0r, h0i, bdr, bdi, cdr, cdi, dvec, a_re, a_im)


def _peer_scores_kernel(x_ref, g_ref, sh_ref, sc_ref, w_ref, key_ref, s_ref, ht_ref, *, nkh, dq):
    h = _norm_mod(x_ref[...], g_ref[...], sh_ref[0], sc_ref[0])
    ht_ref[...] = jnp.transpose(h).astype(BF16)
    q = _dot(h.astype(BF16), w_ref[...]).astype(BF16)
    for c in range(nkh):
        s_ref[c] = _dot_nt(key_ref[c], q[:, c * dq:(c + 1) * dq])


def _peer_scores(x, g, sh, sc, w_q, keys, seq):
    t, d = x.shape
    n = w_q.shape[1]
    nkh, nk, dq = keys.shape
    tm = _row_tile(min(t, seq) if seq > 1 else t, 512)
    return pl.pallas_call(
        functools.partial(_peer_scores_kernel, nkh=nkh, dq=dq),
        out_shape=[jax.ShapeDtypeStruct((nkh, nk, t), F32), jax.ShapeDtypeStruct((d, t), BF16)],
        grid=(t // tm,),
        in_specs=[pl.BlockSpec((tm, d), lambda i: (i, 0)),
                  pl.BlockSpec((1, d), lambda i: (0, 0)),
                  _mod_specs(tm, d, seq),
                  _mod_specs(tm, d, seq),
                  pl.BlockSpec((d, n), lambda i: (0, 0)),
                  pl.BlockSpec((nkh, nk, dq), lambda i: (0, 0, 0))],
        out_specs=[pl.BlockSpec((nkh, nk, tm), lambda i: (0, 0, i)),
                   pl.BlockSpec((d, tm), lambda i: (0, i))],
        compiler_params=_cparams("parallel"),
        name="peer_scores",
    )(x, g, sh, sc, w_q, keys)


def _remove_max_rounds(s, k, idx, want_round):
    cur = s
    rnd = jnp.full(s.shape, float(k), F32) if want_round else None
    vals = []
    for a in range(k):
        m = jnp.max(cur, axis=0, keepdims=True)
        if idx is None:
            hit = cur == m
        else:
            first = jnp.min(jnp.where(cur == m, idx, float(s.shape[0])), axis=0, keepdims=True)
            hit = idx == first
        if want_round:
            rnd = jnp.where(hit, float(a), rnd)
        cur = jnp.where(hit, NEG_INF, cur)
        vals.append(m)
    return vals, rnd, cur


def _rows_from_list(vals, lo, n, ridx):
    out = jnp.zeros(ridx.shape, F32)
    for r in range(n):
        out = jnp.where(ridx == float(r), vals[lo + r], out)
    return out


def _peer_select_one(s1, s2, topk, idx, cidx, valid, ridx_k, ridx_s):
    sub = V7X_SUBLANES
    stair = [topk // (a + 1) for a in range(topk)]
    v1, rank1, cur1 = _remove_max_rounds(s1, topk, idx, idx is not None)
    v2, rank2, cur2 = _remove_max_rounds(s2, topk, idx, True)
    v2all = _rows_from_list(v2, 0, topk, ridx_k)
    v2lo = _rows_from_list(v2, 0, sub, ridx_s)
    v1hi = _rows_from_list(v1, sub, topk - sub, ridx_s)
    blocks = [v1[0] + v2all] + [v1[a] + v2lo for a in range(1, sub)] + [v1hi + v2[0]]
    cand = jnp.where(valid, jnp.concatenate(blocks, axis=0), NEG_INF)
    tops, _, curc = _remove_max_rounds(cand, topk, cidx, False)
    taken = jnp.where(jnp.logical_and(curc == NEG_INF, valid), 1.0, 0.0)
    z = jnp.ones_like(tops[0])
    for m in tops[1:]:
        z = z + jnp.exp(m - tops[0])
    n1 = jnp.zeros(s1.shape, F32)
    total = jnp.zeros_like(z)
    for a in range(topk):
        if a == 0:
            n_a = jnp.sum(taken[0:topk, :], axis=0, keepdims=True)
        elif a < sub:
            base = topk + sub * (a - 1)
            n_a = jnp.sum(taken[base:base + sub, :], axis=0, keepdims=True)
        else:
            row = topk + sub * (sub - 1) + (a - sub)
            n_a = taken[row:row + 1, :]
        total = total + n_a
        n1 = jnp.where((s1 == v1[a]) if idx is None else (rank1 == float(a)), n_a, n1)
    if idx is None:
        k = float(topk)
        c1 = jnp.sum(jnp.where(cur1 == NEG_INF, 1.0, 0.0), axis=0, keepdims=True)
        c2 = jnp.sum(jnp.where(cur2 == NEG_INF, 1.0, 0.0), axis=0, keepdims=True)
        bad = jnp.where(jnp.logical_and(jnp.logical_and(c1 == k, c2 == k), total == k), 0.0, 1.0)
    else:
        bad = None
    e2 = jnp.exp(s2 - v2[0])
    e1 = jnp.exp(s1 - v1[0]) / z
    return rank2, e2, n1, e1, bad


def _peer_select_kernel(s_ref, r2_ref, e2_ref, n1_ref, e1_ref, *, heads, topk):
    sub = V7X_SUBLANES
    nk = s_ref.shape[1]
    lanes = s_ref.shape[2]
    ncand = topk + sub * sub
    idx = lax.broadcasted_iota(jnp.int32, (nk, lanes), 0).astype(F32)
    crow = lax.broadcasted_iota(jnp.int32, (ncand, lanes), 0)
    cidx = crow.astype(F32)
    ridx_k = lax.broadcasted_iota(jnp.int32, (topk, lanes), 0).astype(F32)
    ridx_s = lax.broadcasted_iota(jnp.int32, (sub, lanes), 0).astype(F32)
    valid = crow < topk + sub
    for a in range(2, sub):
        base = topk + sub * (a - 1)
        valid = jnp.logical_or(valid, jnp.logical_and(crow >= base, crow < base + topk // (a + 1)))
    valid = jnp.logical_or(valid, crow >= topk + sub * (sub - 1))

    def write(h, res):
        rank2, e2, n1, e1 = res
        r2_ref[h] = rank2.astype(r2_ref.dtype)
        e2_ref[h] = e2.astype(e2_ref.dtype)
        n1_ref[h] = n1
        e1_ref[h] = e1

    def head_pair(hp, carry):
        hs = (2 * hp, 2 * hp + 1)
        sc = [(s_ref[2 * h], s_ref[2 * h + 1]) for h in hs]
        bad = None
        for h, (s1, s2) in zip(hs, sc):
            *res, b = _peer_select_one(s1, s2, topk, None, None, valid, ridx_k, ridx_s)
            write(h, res)
            bad = b if bad is None else jnp.maximum(bad, b)

        @pl.when(jnp.max(bad) > 0.0)
        def _():
            for h, (s1, s2) in zip(hs, sc):
                *res, _ = _peer_select_one(s1, s2, topk, idx, cidx, valid, ridx_k, ridx_s)
                write(h, res)

        return carry

    lax.fori_loop(0, heads // 2, head_pair, 0)


def _peer_select(scores, heads, topk):
    nkh, nk, t = scores.shape
    tl = _row_tile(t, 2 * V7X_LANES)
    shp = jax.ShapeDtypeStruct((heads, nk, t), F32)
    shp_i2 = jax.ShapeDtypeStruct((heads, nk, t), BF16)
    ospec = pl.BlockSpec((heads, nk, tl), lambda i: (0, 0, i))
    return pl.pallas_call(
        functools.partial(_peer_select_kernel, heads=heads, topk=topk),
        out_shape=[shp_i2, shp_i2, shp, shp],
        grid=(t // tl,),
        in_specs=[pl.BlockSpec((nkh, nk, tl), lambda i: (0, 0, i))],
        out_specs=[ospec, ospec, ospec, ospec],
        compiler_params=_cparams("parallel"),
        name="peer_select",
    )(scores)


def _peer_dense_kernel(ht_ref, u_ref, vt_ref, r2_ref, e2_ref, n1_ref, e1_ref, x_ref, gate_ref, o_ref,
                       at0_scr, at1_scr, wt_scr, acc_scr, *, heads, nk, i1_per_step, nblk):
    j = pl.program_id(1)
    at_scr = (at0_scr, at1_scr)

    def activations(dst):
        dst[...] = _dot(u_ref[...], ht_ref[...])

    pack = 2 * V7X_SUBLANES
    tn = wt_scr.shape[1]

    def packed_row(ref, h, r):
        return jnp.broadcast_to(ref[h, r:r + 1, :], (pack, tn)).astype(BF16)[None]

    def mix(src):
        for r in range(i1_per_step):
            rows = pl.ds(r * nk, nk)
            g = None
            for h in range(heads):
                n = packed_row(n1_ref, h, r)
                e1 = packed_row(e1_ref, h, r)
                r2 = r2_ref[h].reshape(nk // pack, pack, tn)
                e2 = e2_ref[h].reshape(nk // pack, pack, tn)
                term = jnp.where(r2 < n, e2, jnp.zeros_like(e2)) * e1
                g = term if g is None else g + term
            g = g.reshape(nk, tn).astype(F32)
            wt_scr[rows, :] = (_gelu_tanh(src[rows, :]) * g).astype(BF16)
        acc_scr[...] += _dot(vt_ref[0], wt_scr[...])

    @pl.when(j == 0)
    def _():
        acc_scr[...] = jnp.zeros_like(acc_scr)
        activations(at_scr[0])

    for parity in range(2):
        @pl.when(jnp.logical_and(jnp.logical_and(j > 0, j < nblk), j % 2 == parity))
        def _():
            activations(at_scr[parity])
            mix(at_scr[1 - parity])

    @pl.when(j == nblk)
    def _():
        mix(at_scr[(nblk - 1) % 2])
        o_ref[...] = x_ref[...] + gate_ref[0] * jnp.transpose(acc_scr[...])


def _peer_dense(ht, u, vt, r2, e2, n1, e1, x, gate, seq):
    t, d = x.shape
    heads, nk, _ = r2.shape
    i1_per_step = V7X_SUBLANES
    nblk, _, te = vt.shape
    assert te == i1_per_step * nk and u.shape == (nblk * te, d)
    tn = _row_tile(min(t, seq) if seq > 1 else t, 512)
    clamp = lambda b: jnp.clip(b, 0, nblk - 1)
    tok = pl.BlockSpec((heads, nk, tn), lambda i, j: (0, 0, i))
    per_i1 = pl.BlockSpec((heads, i1_per_step, tn), lambda i, j: (0, clamp(j - 1), i))
    return pl.pallas_call(
        functools.partial(_peer_dense_kernel, heads=heads, nk=nk, i1_per_step=i1_per_step, nblk=nblk),
        out_shape=jax.ShapeDtypeStruct((t, d), F32),
        grid=(t // tn, nblk + 1),
        in_specs=[
            pl.BlockSpec((d, tn), lambda i, j: (0, i)),
            pl.BlockSpec((te, d), lambda i, j: (clamp(j), 0)),
            pl.BlockSpec((1, d, te), lambda i, j: (clamp(j - 1), 0, 0)),
            tok, tok, per_i1, per_i1,
            pl.BlockSpec((tn, d), lambda i, j: (i, 0)),
            _mod_specs(tn, d, seq),
        ],
        out_specs=pl.BlockSpec((tn, d), lambda i, j: (i, 0)),
        scratch_shapes=[pltpu.VMEM((te, tn), F32), pltpu.VMEM((te, tn), F32),
                        pltpu.VMEM((te, tn), BF16), pltpu.VMEM((d, tn), F32)],
        compiler_params=_cparams("parallel", "arbitrary"),
        name="peer_dense",
    )(ht, u, vt, r2, e2, n1, e1, x, gate)


def _rmsnorm_kernel(x_ref, g_ref, o_ref):
    x = x_ref[...]
    o_ref[...] = (x * lax.rsqrt(jnp.mean(x * x, axis=-1, keepdims=True) + EPS)) * g_ref[...]


def _rmsnorm(x, g):
    t, d = x.shape
    tm = _row_tile(t, 1024)
    return pl.pallas_call(
        _rmsnorm_kernel,
        out_shape=jax.ShapeDtypeStruct((t, d), F32),
        grid=(t // tm,),
        in_specs=[pl.BlockSpec((tm, d), lambda i: (i, 0)), pl.BlockSpec((1, d), lambda i: (0, 0))],
        out_specs=pl.BlockSpec((tm, d), lambda i: (i, 0)),
        compiler_params=_cparams("parallel"),
        name="final_rmsnorm",
    )(x, g)


def _mods(mod, lo, hi, seq):
    d = mod.shape[1] // 6
    rows = mod[lo:hi]
    parts = [rows[:, k * d:(k + 1) * d] for k in range(6)]
    if seq == 1:
        return [p[None, :, :] for p in parts]
    return [p[:, None, :] for p in parts]


def kernel(x_prompt, x_sample, c_prompt, c_sample, state_ret, state_ssm_re, state_ssm_im, norm_g, final_g, w_ada, b_ada, ret_w_in, ret_w_out, ssm_w_in, ssm_lam_re, ssm_lam_im, ssm_log_dt, ssm_b_re, ssm_b_im, ssm_c_re, ssm_c_im, ssm_d, ssm_w_glu, peer_w_q, peer_key1, peer_key2, peer_u, peer_v):
    batch, seq, d = x_prompt.shape
    dbatch, dseq, _ = x_sample.shape
    assert dseq == 1 and seq % RET_CHUNK == 0
    depth = w_ada.shape[0]
    heads = peer_key1.shape[1]

    c_all = jnp.concatenate([c_prompt, c_sample], axis=0)
    pad = (-c_all.shape[0]) % V7X_SUBLANES
    c_all = jnp.pad(c_all, ((0, pad), (0, 0)))
    mod = _ada(c_all, w_ada, b_ada)

    groups = [
        dict(x=x_prompt.reshape(batch * seq, d), lo=0, hi=batch, seq=seq, nb=batch),
        dict(x=x_sample.reshape(dbatch, d), lo=batch, hi=batch + dbatch, seq=1, nb=dbatch),
    ]
    outs = [dict(ret=[], re=[], im=[]) for _ in groups]
    ret_sample = None

    for i in range(depth):
        jm = i // N_MIXERS
        is_ret = i % N_MIXERS == 0
        w_q = peer_w_q[i].astype(BF16)
        keys = jnp.stack([peer_key1[i], peer_key2[i]], axis=1).reshape(2 * heads, PEER_NKEYS, -1).astype(BF16)
        u_tab = peer_u[i].astype(BF16)
        te = V7X_SUBLANES * PEER_NKEYS
        vt_tab = jnp.transpose(peer_v[i].reshape(-1, te, d), (0, 2, 1)).astype(BF16)
        g_mix = norm_g[i, 0].reshape(1, d)
        g_peer = norm_g[i, 1].reshape(1, d)
        if is_ret:
            w_in = ret_w_in[jm].astype(BF16)
            w_out = ret_w_out[jm].astype(BF16)
        else:
            w_in = ssm_w_in[jm].astype(BF16)
            w_out = ssm_w_glu[jm].astype(BF16)
            a_re, a_im, bbr, bbi = _ssm_discretize(ssm_lam_re[jm], ssm_lam_im[jm], ssm_log_dt[jm],
                                                   ssm_b_re[jm], ssm_b_im[jm])
            per = V7X_MXU_DIM // SSM_GROUP
            disc = (_block_diag(bbr, per).astype(BF16), _block_diag(bbi, per).astype(BF16),
                    _block_diag(jnp.transpose(ssm_c_re[jm], (0, 2, 1)), per).astype(BF16),
                    _block_diag(jnp.transpose(ssm_c_im[jm], (0, 2, 1)), per).astype(BF16),
                    ssm_d[jm].reshape(1, d), a_re.reshape(1, -1), a_im.reshape(1, -1))

        for gi, grp in enumerate(groups):
            x = grp["x"]
            sq = grp["seq"]
            sh1, sc1, g1, sh2, sc2, g2 = _mods(mod[i], grp["lo"], grp["hi"], sq)
            tmaj = sq > 1 and not is_ret
            proj = _nm_matmul(x, g_mix, sh1, sc1, w_in, sq, time_major=tmaj)
            if is_ret:
                if sq > 1:
                    y, s = _retention_prompt(proj, grp["nb"], sq, d)
                    outs[gi]["ret"].append(s)
                else:
                    y, ret_sample = _retention_step(proj, state_ret, jm, ret_sample, d)
                x = _matmul_res(y, w_out, x, g1, sq, glu=False)
            else:
                if sq > 1:
                    z, fr, fi = _ssm_prompt(proj, disc, grp["nb"], sq)
                else:
                    z, fr, fi = _ssm_step(proj, state_ssm_re[jm].reshape(dbatch, -1),
                                          state_ssm_im[jm].reshape(dbatch, -1), disc)
                outs[gi]["re"].append(fr.reshape(grp["nb"], -1, SSM_STATE))
                outs[gi]["im"].append(fi.reshape(grp["nb"], -1, SSM_STATE))
                x = _matmul_res(z, w_out, x, g1, sq, glu=True, a_time_major=tmaj)
            scores, h2 = _peer_scores(x, g_peer, sh2, sc2, w_q, keys, sq)
            r2, e2, n1, e1 = _peer_select(scores, heads, PEER_TOPK)
            x = _peer_dense(h2, u_tab, vt_tab, r2, e2, n1, e1, x, g2, sq)
            grp["x"] = x

    fg = final_g.reshape(1, d)
    y_prompt = _rmsnorm(groups[0]["x"], fg).reshape(batch, seq, d)
    y_sample = _rmsnorm(groups[1]["x"], fg).reshape(dbatch, 1, d)
    return (y_prompt, y_sample,
            jnp.stack(outs[0]["ret"]), ret_sample,
            jnp.stack(outs[0]["re"]), jnp.stack(outs[0]["im"]),
            jnp.stack(outs[1]["re"]), jnp.stack(outs[1]["im"]))
```
